```python
import math
import jax, jax.numpy as jnp
from jax import lax
import numpy as np

D_MODEL = 4096
BATCH = 4
SEQ = 2048
DEPTH = 2
DEC_BATCH = 8
DEC_SEQ = 1
PAST_LEN = 16384
PAGE_SIZE = 128

HEAD_DIM = 128
ATTN_WIDTH = D_MODEL // 2
N_HEADS = ATTN_WIDTH // HEAD_DIM
SSM_WIDTH = D_MODEL - ATTN_WIDTH
SSM_CH = 16
SSM_GROUPS = SSM_WIDTH // SSM_CH
SSM_STATE = 64
D_FF = 256 * ((8 * D_MODEL // 3 + 255) // 256)
N_META = 16
Q_BLOCK = 128
RMS_EPS = 1e-6
LAMBDA_RE_MAX = -1e-4
SB_BIAS_MIN = -8.0
SB_BIAS_MAX = -3.0

kernel_name = "hymba_s5_stickbreaking_macaron_decode_step"


def rms_norm(x, gain):
    xf = x.astype(jnp.float32)
    xf = xf * lax.rsqrt(jnp.mean(xf * xf, axis=-1, keepdims=True) + RMS_EPS)
    return (xf * gain.astype(jnp.float32)).astype(x.dtype)


def half_step_ffn(x, norm_pre, norm_post, w_gate, w_up, w_down):
    h = rms_norm(x, norm_pre)
    f = (jax.nn.silu(h @ w_gate) * (h @ w_up)) @ w_down
    return x + 0.5 * rms_norm(f, norm_post)


def s5_mixer(u, h0_re, h0_im, lam_re, lam_im, log_step, b_re, b_im, c_re, c_im, d, w_glu, b_glu):
    dt = u.dtype
    uf = u.astype(jnp.float32)
    step = jnp.exp(log_step.astype(jnp.float32))[:, None]
    lr = jnp.minimum(lam_re.astype(jnp.float32), LAMBDA_RE_MAX)
    li = lam_im.astype(jnp.float32)
    mag = jnp.exp(lr * step)
    ar, ai = mag * jnp.cos(li * step), mag * jnp.sin(li * step)
    den = lr * lr + li * li
    fr = ((ar - 1.0) * lr + ai * li) / den
    fi = (ai * lr - (ar - 1.0) * li) / den
    br, bi = b_re.astype(jnp.float32), b_im.astype(jnp.float32)
    bbar_re = fr[..., None] * br - fi[..., None] * bi
    bbar_im = fr[..., None] * bi + fi[..., None] * br
    bu_re = jnp.einsum('blgc,gpc->blgp', uf, bbar_re)
    bu_im = jnp.einsum('blgc,gpc->blgp', uf, bbar_im)
    s0r, s0i = h0_re.astype(jnp.float32), h0_im.astype(jnp.float32)
    bu_re = bu_re.at[:, 0].add(ar * s0r - ai * s0i)
    bu_im = bu_im.at[:, 0].add(ar * s0i + ai * s0r)
    a_re = jnp.broadcast_to(ar, bu_re.shape)
    a_im = jnp.broadcast_to(ai, bu_im.shape)

    def combine(e1, e2):
        a1r, a1i, b1r, b1i = e1
        a2r, a2i, b2r, b2i = e2
        return (a2r * a1r - a2i * a1i, a2r * a1i + a2i * a1r,
                a2r * b1r - a2i * b1i + b2r, a2r * b1i + a2i * b1r + b2i)

    _, _, sr, si = lax.associative_scan(combine, (a_re, a_im, bu_re, bu_im), axis=1)
    y = (jnp.einsum('blgp,gcp->blgc', sr, c_re.astype(jnp.float32))
         - jnp.einsum('blgp,gcp->blgc', si, c_im.astype(jnp.float32))
         + d.astype(jnp.float32) * uf)
    g = jax.nn.gelu(y)
    out = g * jax.nn.sigmoid(jnp.einsum('blgc,gcd->blgd', g, w_glu.astype(jnp.float32)) + b_glu.astype(jnp.float32))
    return out.astype(dt), sr[:, -1].astype(h0_re.dtype), si[:, -1].astype(h0_im.dtype)


def stick_breaking_attention(q, k, v, bias, q_pos, k_pos, k_valid):
    b, lq, h, dh = q.shape
    qb = Q_BLOCK if lq % Q_BLOCK == 0 else lq
    nb = lq // qb
    q_blocks = q.reshape(b, nb, qb, h, dh).transpose(1, 0, 2, 3, 4)
    p_blocks = q_pos.reshape(nb, qb)
    scale = 1.0 / math.sqrt(dh)
    bias_f = bias.astype(jnp.float32)[None, :, None, None]

    def one_block(args):
        q_blk, p_blk = args
        z = jnp.einsum('bqhd,bkhd->bhqk', q_blk, k, preferred_element_type=jnp.float32) * scale + bias_f
        allowed = (k_pos[None, :] < p_blk[:, None]) & k_valid[None, :]
        log_1m = jnp.where(allowed, -jax.nn.softplus(z), 0.0)
        after = lax.cumsum(log_1m, axis=3, reverse=True) - log_1m
        a = jnp.where(allowed, jnp.exp(jax.nn.log_sigmoid(z) + after), 0.0)
        return jnp.einsum('bhqk,bkhd->bqhd', a.astype(v.dtype), v, preferred_element_type=jnp.float32).astype(q.dtype)

    out = lax.map(one_block, (q_blocks, p_blocks))
    return out.transpose(1, 0, 2, 3, 4).reshape(b, lq, h, dh)


def prompt_attend(q, k, v, bias):
    pad = (-N_META) % Q_BLOCK
    l = q.shape[1]
    widths = ((0, 0), (pad, 0), (0, 0), (0, 0))
    qp, kp, vp = jnp.pad(q, widths), jnp.pad(k, widths), jnp.pad(v, widths)
    pos = jnp.arange(l + pad, dtype=jnp.int32) - pad
    out = stick_breaking_attention(qp, kp, vp, bias, pos, pos, pos >= 0)
    return out[:, pad:]


def make_sample_attend(k_pages, v_pages, page_table):
    def attend(q, k, v, bias):
        n_b, n_pages = page_table.shape
        past = n_pages * PAGE_SIZE
        k_past = k_pages[page_table].reshape(n_b, past, N_HEADS, HEAD_DIM)
        v_past = v_pages[page_table].reshape(n_b, past, N_HEADS, HEAD_DIM)
        k_all = jnp.concatenate([k_past, k.astype(k_past.dtype)], axis=1)
        v_all = jnp.concatenate([v_past, v.astype(v_past.dtype)], axis=1)
        q_pos = past + jnp.arange(q.shape[1], dtype=jnp.int32)
        k_pos = jnp.concatenate([jnp.arange(past, dtype=jnp.int32), q_pos])
        return stick_breaking_attention(q, k_all, v_all, bias, q_pos, k_pos, jnp.ones(k_pos.shape, dtype=bool))
    return attend


def token_mixing(x, lw, h0_re, h0_im, attend):
    b, l, _ = x.shape
    h = rms_norm(x, lw['mix_norm_pre'])
    proj = h @ lw['w_in']
    u = proj[..., :SSM_WIDTH].reshape(b, l, SSM_GROUPS, SSM_CH)
    q, k, v = jnp.split(proj[..., SSM_WIDTH:], 3, axis=-1)
    q = q.reshape(b, l, N_HEADS, HEAD_DIM)
    k = k.reshape(b, l, N_HEADS, HEAD_DIM)
    v = v.reshape(b, l, N_HEADS, HEAD_DIM)
    ssm_y, s_re, s_im = s5_mixer(u, h0_re, h0_im, lw['lam_re'], lw['lam_im'], lw['log_step'],
                                 lw['b_re'], lw['b_im'], lw['c_re'], lw['c_im'], lw['d'],
                                 lw['w_glu'], lw['b_glu'])
    attn_y = attend(q, k, v, lw['sb_bias'])
    merged = jnp.concatenate([rms_norm(ssm_y.reshape(b, l, SSM_WIDTH), lw['ssm_out_norm']),
                              rms_norm(attn_y.reshape(b, l, ATTN_WIDTH), lw['attn_out_norm'])], axis=-1)
    x = x + rms_norm(merged @ lw['w_out'], lw['mix_norm_post'])
    return x, k, v, s_re, s_im


def decoder_layer(x, lw, h0_re, h0_im, attend):
    x = half_step_ffn(x, lw['ffn1_norm_pre'], lw['ffn1_norm_post'], lw['ffn1_w_gate'], lw['ffn1_w_up'], lw['ffn1_w_down'])
    x, k, v, s_re, s_im = token_mixing(x, lw, h0_re, h0_im, attend)
    x = half_step_ffn(x, lw['ffn2_norm_pre'], lw['ffn2_norm_post'], lw['ffn2_w_gate'], lw['ffn2_w_up'], lw['ffn2_w_down'])
    return x, k, v, s_re, s_im


def setup_inputs(seed: int = 0) -> dict:
    key = jax.random.key(seed)
    ks = jax.random.split(key, 40)
    f32 = jnp.float32
    n_pages = PAST_LEN // PAGE_SIZE
    n_used = DEC_BATCH * n_pages
    n_phys = n_used + n_used // 4
    nrm = lambda k, shape, s: jax.random.normal(k, shape, f32) * s
    gain = lambda k, shape: 1.0 + 0.01 * jax.random.normal(k, shape, f32)
    page_table = jax.random.permutation(ks[6], n_phys)[:n_used].reshape(DEC_BATCH, n_pages).astype(jnp.int32)
    lam_im = math.pi * jnp.arange(SSM_STATE, dtype=f32)[None, None, :] + nrm(ks[16], (DEPTH, SSM_GROUPS, SSM_STATE), 0.01)
    log_step = jax.random.uniform(ks[17], (DEPTH, SSM_GROUPS), f32, math.log(1e-3), math.log(1e-1))
    return {
        'x_prompt': nrm(ks[0], (BATCH, SEQ, D_MODEL), 1.0),
        'x_sample': nrm(ks[1], (DEC_BATCH, DEC_SEQ, D_MODEL), 1.0),
        'cache_k': nrm(ks[2], (DEPTH, n_phys, PAGE_SIZE, N_HEADS, HEAD_DIM), 1.0),
        'cache_v': nrm(ks[3], (DEPTH, n_phys, PAGE_SIZE, N_HEADS, HEAD_DIM), 1.0),
        'state_ssm_re': nrm(ks[4], (DEPTH, DEC_BATCH, SSM_GROUPS, SSM_STATE), 0.5),
        'state_ssm_im': nrm(ks[5], (DEPTH, DEC_BATCH, SSM_GROUPS, SSM_STATE), 0.5),
        'page_table': page_table,
        'meta_tokens': nrm(ks[7], (N_META, D_MODEL), 1.0),
        'ffn1_norm_pre': gain(ks[8], (DEPTH, D_MODEL)),
        'ffn1_norm_post': gain(ks[9], (DEPTH, D_MODEL)),
        'ffn1_w_gate': nrm(ks[10], (DEPTH, D_MODEL, D_FF), D_MODEL ** -0.5),
        'ffn1_w_up': nrm(ks[11], (DEPTH, D_MODEL, D_FF), D_MODEL ** -0.5),
        'ffn1_w_down': nrm(ks[12], (DEPTH, D_FF, D_MODEL), D_FF ** -0.5),
        'mix_norm_pre': gain(ks[13], (DEPTH, D_MODEL)),
        'mix_norm_post': gain(ks[14], (DEPTH, D_MODEL)),
        'w_in': nrm(ks[15], (DEPTH, D_MODEL, SSM_WIDTH + 3 * ATTN_WIDTH), D_MODEL ** -0.5),
        'sb_logit_bias': jax.random.uniform(ks[34], (DEPTH, N_HEADS), f32, SB_BIAS_MIN, SB_BIAS_MAX),
        'ssm_lambda_re': -0.5 + nrm(ks[18], (DEPTH, SSM_GROUPS, SSM_STATE), 0.01),
        'ssm_lambda_im': lam_im,
        'ssm_log_step': log_step,
        'ssm_b_re': nrm(ks[19], (DEPTH, SSM_GROUPS, SSM_STATE, SSM_CH), (2.0 * SSM_CH) ** -0.5),
        'ssm_b_im': nrm(ks[20], (DEPTH, SSM_GROUPS, SSM_STATE, SSM_CH), (2.0 * SSM_CH) ** -0.5),
        'ssm_c_re': nrm(ks[21], (DEPTH, SSM_GROUPS, SSM_CH, SSM_STATE), (2.0 * SSM_STATE) ** -0.5),
        'ssm_c_im': nrm(ks[22], (DEPTH, SSM_GROUPS, SSM_CH, SSM_STATE), (2.0 * SSM_STATE) ** -0.5),
        'ssm_d': nrm(ks[23], (DEPTH, SSM_GROUPS, SSM_CH), 1.0),
        'ssm_w_glu': nrm(ks[24], (DEPTH, SSM_GROUPS, SSM_CH, SSM_CH), SSM_CH ** -0.5),
        'ssm_b_glu': nrm(ks[25], (DEPTH, SSM_GROUPS, SSM_CH), 0.01),
        'ssm_out_norm': gain(ks[26], (DEPTH, SSM_WIDTH)),
        'attn_out_norm': gain(ks[27], (DEPTH, ATTN_WIDTH)),
        'w_out': nrm(ks[28], (DEPTH, SSM_WIDTH + ATTN_WIDTH, D_MODEL), D_MODEL ** -0.5),
        'ffn2_norm_pre': gain(ks[29], (DEPTH, D_MODEL)),
        'ffn2_norm_post': gain(ks[30], (DEPTH, D_MODEL)),
        'ffn2_w_gate': nrm(ks[31], (DEPTH, D_MODEL, D_FF), D_MODEL ** -0.5),
        'ffn2_w_up': nrm(ks[32], (DEPTH, D_MODEL, D_FF), D_MODEL ** -0.5),
        'ffn2_w_down': nrm(ks[33], (DEPTH, D_FF, D_MODEL), D_FF ** -0.5),
    }


def reference(x_prompt, x_sample, cache_k, cache_v, state_ssm_re, state_ssm_im, page_table, meta_tokens,
              ffn1_norm_pre, ffn1_norm_post, ffn1_w_gate, ffn1_w_up, ffn1_w_down,
              mix_norm_pre, mix_norm_post, w_in, sb_logit_bias,
              ssm_lambda_re, ssm_lambda_im, ssm_log_step, ssm_b_re, ssm_b_im, ssm_c_re, ssm_c_im,
              ssm_d, ssm_w_glu, ssm_b_glu, ssm_out_norm, attn_out_norm, w_out,
              ffn2_norm_pre, ffn2_norm_post, ffn2_w_gate, ffn2_w_up, ffn2_w_down):
    b = x_prompt.shape[0]
    meta = jnp.broadcast_to(meta_tokens[None].astype(x_prompt.dtype), (b, N_META, D_MODEL))
    xp = jnp.concatenate([meta, x_prompt], axis=1)
    xs = x_sample
    zero_state = jnp.zeros((b, SSM_GROUPS, SSM_STATE), dtype=state_ssm_re.dtype)
    kp_l, vp_l, srp_l, sip_l, ks_l, vs_l, srs_l, sis_l = [], [], [], [], [], [], [], []
    for i in range(DEPTH):
        lw = dict(
            ffn1_norm_pre=ffn1_norm_pre[i], ffn1_norm_post=ffn1_norm_post[i],
            ffn1_w_gate=ffn1_w_gate[i], ffn1_w_up=ffn1_w_up[i], ffn1_w_down=ffn1_w_down[i],
            mix_norm_pre=mix_norm_pre[i], mix_norm_post=mix_norm_post[i], w_in=w_in[i],
            sb_bias=sb_logit_bias[i],
            lam_re=ssm_lambda_re[i], lam_im=ssm_lambda_im[i], log_step=ssm_log_step[i],
            b_re=ssm_b_re[i], b_im=ssm_b_im[i], c_re=ssm_c_re[i], c_im=ssm_c_im[i], d=ssm_d[i],
            w_glu=ssm_w_glu[i], b_glu=ssm_b_glu[i], ssm_out_norm=ssm_out_norm[i],
            attn_out_norm=attn_out_norm[i], w_out=w_out[i],
            ffn2_norm_pre=ffn2_norm_pre[i], ffn2_norm_post=ffn2_norm_post[i],
            ffn2_w_gate=ffn2_w_gate[i], ffn2_w_up=ffn2_w_up[i], ffn2_w_down=ffn2_w_down[i])
        xp, kp, vp, srp, sip = decoder_layer(xp, lw, zero_state, zero_state, prompt_attend)
        sample_attend = make_sample_attend(cache_k[i], cache_v[i], page_table)
        xs, ksm, vsm, srs, sis = decoder_layer(xs, lw, state_ssm_re[i], state_ssm_im[i], sample_attend)
        kp_l.append(kp); vp_l.append(vp); srp_l.append(srp); sip_l.append(sip)
        ks_l.append(ksm); vs_l.append(vsm); srs_l.append(srs); sis_l.append(sis)
    y_prompt = xp[:, N_META:]
    y_sample = xs
    k_prompt, v_prompt = jnp.stack(kp_l), jnp.stack(vp_l)
    ssm_re_prompt, ssm_im_prompt = jnp.stack(srp_l), jnp.stack(sip_l)
    k_sample, v_sample = jnp.stack(ks_l), jnp.stack(vs_l)
    ssm_re_sample, ssm_im_sample = jnp.stack(srs_l), jnp.stack(sis_l)
    return (y_prompt, y_sample, k_prompt, v_prompt, ssm_re_prompt, ssm_im_prompt,
            k_sample, v_sample, ssm_re_sample, ssm_im_sample)
```

```python
import functools
import math

import numpy as np
import jax
import jax.numpy as jnp
from jax import lax
from jax.experimental import pallas as pl
from jax.experimental.pallas import tpu as pltpu

F32 = jnp.float32
BF16 = jnp.bfloat16

D_MODEL = 4096
BATCH = 4
SEQ = 2048
DEPTH = 2
DEC_BATCH = 8
PAGE_SIZE = 128
N_HEADS = 16
HEAD_DIM = 128
ATTN_WIDTH = N_HEADS * HEAD_DIM
SSM_CH = 16
SSM_GROUPS = 128
SSM_STATE = 64
SSM_WIDTH = SSM_GROUPS * SSM_CH
D_FF = 11008
N_META = 16
RMS_EPS = 1e-6
LAMBDA_RE_MAX = -1e-4
ATTN_SCALE = 1.0 / math.sqrt(HEAD_DIM)

ROWS_MAIN = BATCH * SEQ
TAIL_ROWS = 128
ROWS = ROWS_MAIN + TAIL_ROWS
META_OFF = TAIL_ROWS - N_META

GB = 16
N_GB = SSM_GROUPS // GB
GB_CH = GB * SSM_CH
GB_ST = GB * SSM_STATE

TM = 640
VMEM_LIMIT = 56 * 1024 * 1024


def _cparams(sem, vmem=VMEM_LIMIT):
    return pltpu.CompilerParams(dimension_semantics=sem, vmem_limit_bytes=vmem)


def _rms(x, gain):
    r = lax.rsqrt(jnp.mean(x * x, axis=-1, keepdims=True) + RMS_EPS)
    return x * r * gain


def _rmsnorm_kernel(x_ref, g_ref, o_ref):
    o_ref[...] = _rms(x_ref[...], g_ref[...]).astype(o_ref.dtype)


def rmsnorm_rows(x, gain, out_dtype=BF16, tm=128):
    rows, d = x.shape
    return pl.pallas_call(
        _rmsnorm_kernel,
        out_shape=jax.ShapeDtypeStruct((rows, d), out_dtype),
        grid=(rows // tm,),
        in_specs=[pl.BlockSpec((tm, d), lambda i: (i, 0)),
                  pl.BlockSpec((1, d), lambda i: (0, 0))],
        out_specs=pl.BlockSpec((tm, d), lambda i: (i, 0)),
        compiler_params=_cparams(("parallel",)),
        name="rmsnorm_rows",
    )(x, gain.reshape(1, d))


def _gateup_kernel(h_ref, wg_ref, wu_ref, o_ref, wgb_ref, wub_ref):
    @pl.when(pl.program_id(1) == 0)
    def _():
        wgb_ref[...] = wg_ref[...].astype(BF16)
        wub_ref[...] = wu_ref[...].astype(BF16)

    h = h_ref[...]
    g = jnp.dot(h, wgb_ref[...], preferred_element_type=F32)
    u = jnp.dot(h, wub_ref[...], preferred_element_type=F32)
    o_ref[...] = (g * jax.nn.sigmoid(g) * u).astype(o_ref.dtype)


def gateup(h, wg, wu, layer, tm=TM, tf=256):
    rows, d = h.shape
    f = wg.shape[2]
    w_spec = pl.BlockSpec((None, d, tf), lambda j, i: (layer, 0, j))
    return pl.pallas_call(
        _gateup_kernel,
        out_shape=jax.ShapeDtypeStruct((rows, f), BF16),
        grid=(f // tf, rows // tm),
        in_specs=[pl.BlockSpec((tm, d), lambda j, i: (i, 0)), w_spec, w_spec],
        out_specs=pl.BlockSpec((tm, tf), lambda j, i: (i, j)),
        scratch_shapes=[pltpu.VMEM((d, tf), BF16), pltpu.VMEM((d, tf), BF16)],
        compiler_params=_cparams(("arbitrary", "arbitrary")),
        name="ffn_gateup",
    )(h, wg, wu)


def _mm_kernel(a_ref, w_ref, o_ref):
    o_ref[...] = jnp.dot(a_ref[...], w_ref[...], preferred_element_type=F32).astype(o_ref.dtype)


def matmul(a, w, tm=TM, tn=256, out_dtype=F32):
    rows, k = a.shape
    n = w.shape[1]
    return pl.pallas_call(
        _mm_kernel,
        out_shape=jax.ShapeDtypeStruct((rows, n), out_dtype),
        grid=(rows // tm, n // tn),
        in_specs=[pl.BlockSpec((tm, k), lambda i, j: (i, 0)),
                  pl.BlockSpec((k, tn), lambda i, j: (0, j))],
        out_specs=pl.BlockSpec((tm, tn), lambda i, j: (i, j)),
        compiler_params=_cparams(("parallel", "arbitrary")),
        name="matmul",
    )(a, w)


def _mm_wcast_kernel(a_ref, w_ref, o_ref, wb_ref):
    @pl.when(pl.program_id(1) == 0)
    def _():
        wb_ref[...] = w_ref[...].astype(BF16)

    o_ref[...] = jnp.dot(a_ref[...], wb_ref[...], preferred_element_type=F32)


def matmul_wf32(a, w, layer, tm=TM, tn=512):
    rows, k = a.shape
    n = w.shape[2]
    return pl.pallas_call(
        _mm_wcast_kernel,
        out_shape=jax.ShapeDtypeStruct((rows, n), F32),
        grid=(n // tn, rows // tm),
        in_specs=[pl.BlockSpec((tm, k), lambda j, i: (i, 0)),
                  pl.BlockSpec((None, k, tn), lambda j, i: (layer, 0, j))],
        out_specs=pl.BlockSpec((tm, tn), lambda j, i: (i, j)),
        scratch_shapes=[pltpu.VMEM((k, tn), BF16)],
        compiler_params=_cparams(("arbitrary", "arbitrary")),
        name="matmul_wf32",
    )(a, w)


def _mm2_kernel(a1_ref, a2_ref, w1_ref, w2_ref, o_ref, w1b_ref, w2b_ref):
    @pl.when(pl.program_id(1) == 0)
    def _():
        w1b_ref[...] = w1_ref[...].astype(BF16)
        w2b_ref[...] = w2_ref[...].astype(BF16)

    acc = jnp.dot(a1_ref[...], w1b_ref[...], preferred_element_type=F32)
    acc = acc + jnp.dot(a2_ref[...], w2b_ref[...], preferred_element_type=F32)
    o_ref[...] = acc


def matmul2(a1, a2, w, layer, tm=TM, tn=512):
    rows, k1 = a1.shape
    k2 = a2.shape[1]
    assert k1 == k2
    n = w.shape[2]
    return pl.pallas_call(
        _mm2_kernel,
        out_shape=jax.ShapeDtypeStruct((rows, n), F32),
        grid=(n // tn, rows // tm),
        in_specs=[pl.BlockSpec((tm, k1), lambda j, i: (i, 0)),
                  pl.BlockSpec((tm, k2), lambda j, i: (i, 0)),
                  pl.BlockSpec((None, k1, tn), lambda j, i: (layer, 0, j)),
                  pl.BlockSpec((None, k2, tn), lambda j, i: (layer, 1, j))],
        out_specs=pl.BlockSpec((tm, tn), lambda j, i: (i, j)),
        scratch_shapes=[pltpu.VMEM((k1, tn), BF16), pltpu.VMEM((k2, tn), BF16)],
        compiler_params=_cparams(("arbitrary", "arbitrary")),
        name="matmul2",
    )(a1, a2, w, w)


def _resnorm_kernel(f_ref, x_ref, gp_ref, gn_ref, xo_ref, ho_ref, *, scale):
    xn = x_ref[...] + scale * _rms(f_ref[...], gp_ref[...])
    xo_ref[...] = xn
    ho_ref[...] = _rms(xn, gn_ref[...]).astype(ho_ref.dtype)


def resnorm(f, x, gain_post, gain_next, scale, tm=128):
    rows, d = x.shape
    row_spec = pl.BlockSpec((tm, d), lambda i: (i, 0))
    vec_spec = pl.BlockSpec((1, d), lambda i: (0, 0))
    return pl.pallas_call(
        functools.partial(_resnorm_kernel, scale=scale),
        out_shape=(jax.ShapeDtypeStruct((rows, d), F32), jax.ShapeDtypeStruct((rows, d), BF16)),
        grid=(rows // tm,),
        in_specs=[row_spec, row_spec, vec_spec, vec_spec],
        out_specs=(row_spec, row_spec),
        compiler_params=_cparams(("parallel",)),
        name="resnorm",
    )(f, x, gain_post.reshape(1, d), gain_next.reshape(1, d))


def _ssm_discretize_kernel(lre_ref, lim_ref, ls_ref, bre_ref, bim_ref,
                           ar_ref, ai_ref, bbr_ref, bbi_ref):
    step = jnp.exp(ls_ref[...])
    lr = jnp.minimum(lre_ref[...], LAMBDA_RE_MAX)
    li = lim_ref[...]
    mag = jnp.exp(lr * step)
    ar = mag * jnp.cos(li * step)
    ai = mag * jnp.sin(li * step)
    den = lr * lr + li * li
    fr = ((ar - 1.0) * lr + ai * li) / den
    fi = (ai * lr - (ar - 1.0) * li) / den
    br = bre_ref[...]
    bi = bim_ref[...]
    ar_ref[...] = ar
    ai_ref[...] = ai
    bbr_ref[...] = fr * br - fi * bi
    bbi_ref[...] = fr * bi + fi * br


def ssm_discretize(lam_re, lam_im, log_step, b_re, b_im):
    g, p = lam_re.shape
    ch = b_re.shape[1]
    gp = jax.ShapeDtypeStruct((g, 1, p), F32)
    gcp = jax.ShapeDtypeStruct((g, ch, p), F32)
    return pl.pallas_call(
        _ssm_discretize_kernel,
        out_shape=(gp, gp, gcp, gcp),
        name="ssm_discretize",
    )(lam_re.reshape(g, 1, p), lam_im.reshape(g, 1, p), log_step.reshape(g, 1, 1), b_re, b_im)


def _block_diag(x):
    _, r, c = x.shape
    x = x.reshape(N_GB, GB, r, c)
    eye = jnp.eye(GB, dtype=x.dtype)
    return jnp.einsum('bgrc,gh->bgrhc', x, eye).reshape(N_GB, GB * r, GB * c)


SCAN_ROWS = 8
SCAN_TC = 64


def _perm_matrix(n_seq, tc):
    p = np.zeros((SCAN_ROWS * tc, n_seq * tc), np.float32)
    t, b = np.meshgrid(np.arange(tc), np.arange(n_seq), indexing='ij')
    p[(t * SCAN_ROWS + b).ravel(), (b * tc + t).ravel()] = 1.0
    return p


def _ssm_kernel(*refs, n_seq, tc, perm):
    nb = SCAN_ROWS
    n_u = n_seq if perm else 1
    u_refs = refs[:n_u]
    i = n_u
    if perm:
        p_ref, pt_ref = refs[i], refs[i + 1]
        i += 2
    (s0r_ref, s0i_ref, ar_ref, ai_ref, bb_ref, cc_ref, glu_ref, d_ref, bglu_ref, gain_ref) = refs[i:i + 10]
    i += 10
    y_refs = refs[i:i + n_u]
    i += n_u
    str_ref, sti_ref = refs[i], refs[i + 1]
    uhi_ref, ulo_ref, bu_ref, ss_ref, yacc_ref, carry_ref = refs[i + 2:]

    ti = pl.program_id(0)
    be = pl.program_id(1)

    @pl.when(be == 0)
    def _():
        if perm:
            ucat = jnp.concatenate([r[...] for r in u_refs], axis=0)
        else:
            ucat = u_refs[0][...]
        hi = ucat.astype(BF16)
        lo = (ucat - hi.astype(F32)).astype(BF16)
        if perm:
            hi = jnp.dot(p_ref[...], hi, preferred_element_type=F32).astype(BF16)
            lo = jnp.dot(p_ref[...], lo, preferred_element_type=F32).astype(BF16)
        for k in range(N_GB):
            uhi_ref[k] = hi[:, k * GB_CH:(k + 1) * GB_CH]
            ulo_ref[k] = lo[:, k * GB_CH:(k + 1) * GB_CH]

    @pl.when(ti == 0)
    def _():
        carry_ref[be] = jnp.concatenate([s0r_ref[...], s0i_ref[...]], axis=-1)

    uh = uhi_ref[be]
    bu_ref[...] = jnp.dot(uh, bb_ref[...], preferred_element_type=F32)

    a_re = jnp.broadcast_to(ar_ref[...], (nb, GB_ST))
    a_im = jnp.broadcast_to(ai_ref[...], (nb, GB_ST))
    s0 = carry_ref[be]

    def step(t, c):
        sr, si = c
        r0 = pl.multiple_of(t * nb, nb)
        b = bu_ref[pl.ds(r0, nb), :]
        nsr = a_re * sr - a_im * si + b[:, :GB_ST]
        nsi = a_re * si + a_im * sr + b[:, GB_ST:]
        ss_ref[pl.ds(r0, nb), :] = jnp.concatenate([nsr, nsi], axis=-1)
        return nsr, nsi

    sr, si = lax.fori_loop(0, tc, step, (s0[:, :GB_ST], s0[:, GB_ST:]))
    carry_ref[be] = jnp.concatenate([sr, si], axis=-1)
    str_ref[be] = sr
    sti_ref[be] = si

    y = jnp.dot(ss_ref[...].astype(BF16), cc_ref[...], preferred_element_type=F32)
    u32 = uh.astype(F32) + ulo_ref[be].astype(F32)
    y = y + d_ref[...] * u32
    g = jax.nn.gelu(y)
    gate = jnp.dot(g.astype(BF16), glu_ref[...], preferred_element_type=F32) + bglu_ref[...]
    yacc_ref[be] = g * jax.nn.sigmoid(gate)

    @pl.when(be == N_GB - 1)
    def _():
        o = jnp.concatenate([yacc_ref[k] for k in range(N_GB)], axis=-1)
        on = _rms(o, gain_ref[...]).astype(BF16)
        if perm:
            on = jnp.dot(pt_ref[...], on, preferred_element_type=F32).astype(BF16)
            for b in range(n_seq):
                y_refs[b][...] = on[b * tc:(b + 1) * tc]
        else:
            y_refs[0][...] = on


def ssm_mixer(u_src, u_row_block, s0_re, s0_im, sp, *, n_seq, tc, nt, perm):
    nb = SCAN_ROWS
    rows = nb * tc
    n_u = n_seq if perm else 1
    blk_rows = tc if perm else rows
    in_specs = [pl.BlockSpec((blk_rows, SSM_WIDTH), functools.partial(lambda ti, be, b: (u_row_block(b, ti), 0), b=b))
                for b in range(n_u)]
    args = [u_src] * n_u
    if perm:
        pm = _perm_matrix(n_seq, tc)
        in_specs += [pl.BlockSpec(pm.shape, lambda ti, be: (0, 0)), pl.BlockSpec(pm.T.shape, lambda ti, be: (0, 0))]
        args += [jnp.asarray(pm, BF16), jnp.asarray(pm.T, BF16)]
    st_spec = pl.BlockSpec((nb, GB_ST), lambda ti, be: (0, be))
    in_specs += [
        st_spec, st_spec,
        pl.BlockSpec((None, 1, GB_ST), lambda ti, be: (be, 0, 0)),
        pl.BlockSpec((None, 1, GB_ST), lambda ti, be: (be, 0, 0)),
        pl.BlockSpec((None, GB_CH, 2 * GB_ST), lambda ti, be: (be, 0, 0)),
        pl.BlockSpec((None, 2 * GB_ST, GB_CH), lambda ti, be: (be, 0, 0)),
        pl.BlockSpec((None, GB_CH, GB_CH), lambda ti, be: (be, 0, 0)),
        pl.BlockSpec((None, 1, GB_CH), lambda ti, be: (be, 0, 0)),
        pl.BlockSpec((None, 1, GB_CH), lambda ti, be: (be, 0, 0)),
        pl.BlockSpec((1, SSM_WIDTH), lambda ti, be: (0, 0)),
    ]
    args += [s0_re, s0_im, sp['ar'], sp['ai'], sp['bb'], sp['cc'], sp['glu'], sp['d'], sp['bglu'], sp['gain']]
    y_shape = jax.ShapeDtypeStruct((nt * blk_rows, SSM_WIDTH), BF16)
    st_shape = jax.ShapeDtypeStruct((N_GB, nb, GB_ST), F32)
    st_out_spec = pl.BlockSpec((N_GB, nb, GB_ST), lambda ti, be: (0, 0, 0))
    out = pl.pallas_call(
        functools.partial(_ssm_kernel, n_seq=n_seq, tc=tc, perm=perm),
        out_shape=tuple([y_shape] * n_u + [st_shape, st_shape]),
        grid=(nt, N_GB),
        in_specs=in_specs,
        out_specs=tuple([pl.BlockSpec((blk_rows, SSM_WIDTH), lambda ti, be: (ti, 0))] * n_u + [st_out_spec, st_out_spec]),
        scratch_shapes=[
            pltpu.VMEM((N_GB, rows, GB_CH), BF16),
            pltpu.VMEM((N_GB, rows, GB_CH), BF16),
            pltpu.VMEM((rows, 2 * GB_ST), F32),
            pltpu.VMEM((rows, 2 * GB_ST), F32),
            pltpu.VMEM((N_GB, rows, GB_CH), F32),
            pltpu.VMEM((N_GB, nb, 2 * GB_ST), F32),
        ],
        compiler_params=_cparams(("arbitrary", "arbitrary")),
        name="ssm_mixer",
    )(*args)
    flat = lambda s: jnp.swapaxes(s, 0, 1).reshape(nb, N_GB * GB_ST)
    return out[:n_u], flat(out[n_u]), flat(out[n_u + 1])


def _softplus(z):
    return jnp.maximum(z, 0.0) + jnp.log(1.0 + jnp.exp(-jnp.abs(z)))


def _split_bf16(x):
    hi = x.astype(BF16)
    return hi, (x - hi.astype(F32)).astype(BF16)


def _sb_block(q, kb, vb, upper, bias, run, acc, mask):
    z = lax.dot_general(q, kb.astype(BF16), (((1,), (1,)), ((), ())), preferred_element_type=F32)
    z = z * ATTN_SCALE + bias
    sp = _softplus(z)
    log_1m = -sp
    if mask is not None:
        log_1m = jnp.where(mask, log_1m, 0.0)
    hi, lo = _split_bf16(log_1m)
    after = (jnp.dot(hi, upper, preferred_element_type=F32)
             + jnp.dot(lo, upper, preferred_element_type=F32))
    a = jnp.exp((z - sp) + after + run)
    if mask is not None:
        a = jnp.where(mask, a, 0.0)
    acc = acc + jnp.dot(a.astype(BF16), vb.astype(BF16), preferred_element_type=F32)
    run = run + jnp.sum(log_1m, axis=1, keepdims=True)
    return run, acc


def _upper_matrix(n):
    j, s = np.meshgrid(np.arange(n), np.arange(n), indexing='ij')
    return (j > s).astype(np.float32)


def _upper_ext_matrix(n):
    return np.concatenate([_upper_matrix(n), np.ones((n, HEAD_DIM), np.float32)], axis=1)


LOG2E = 1.4426950408889634


def _sb_block_ext(q2, kb, vb, neg_upper_ext, bias2, run, mask):
    tk = kb.shape[0]
    w = lax.dot_general(q2, kb.astype(BF16), (((1,), (1,)), ((), ())), preferred_element_type=F32) + bias2
    sp2 = jnp.maximum(w, 0.0) + jnp.log2(1.0 + jnp.exp2(-jnp.abs(w)))
    if mask is not None:
        sp2 = jnp.where(mask, sp2, 0.0)
    ext = jnp.dot(sp2.astype(BF16), neg_upper_ext, preferred_element_type=F32)
    run_wide = jnp.concatenate([run] * (tk // HEAD_DIM), axis=1)
    a = jnp.exp2((w - sp2) + ext[:, :tk] + run_wide)
    if mask is not None:
        a = jnp.where(mask, a, 0.0)
    pv = jnp.dot(a.astype(BF16), vb.astype(BF16), preferred_element_type=F32)
    return run + ext[:, tk:], pv


HEADS_PER_STEP = 4


def _prompt_attn_kernel(bias_ref, q_ref, k_ref, v_ref, km_ref, vm_ref, up_ref, upm_ref, o_ref, run_ref, *, tb):
    hb = HEADS_PER_STEP
    hg = pl.program_id(1)
    iq = pl.program_id(2)
    heads = [slice(hh * HEAD_DIM, (hh + 1) * HEAD_DIM) for hh in range(hb)]
    bias = [bias_ref[hg * hb + hh] * LOG2E for hh in range(hb)]
    q = [(q_ref[:, sl] * (ATTN_SCALE * LOG2E)).astype(BF16) for sl in heads]
    row = lax.broadcasted_iota(jnp.int32, (tb, tb), 0)
    col = lax.broadcasted_iota(jnp.int32, (tb, tb), 1)

    d0 = pl.multiple_of(iq * tb, tb)
    for hh, sl in enumerate(heads):
        run, pv = _sb_block_ext(q[hh], k_ref[pl.ds(d0, tb), sl], v_ref[pl.ds(d0, tb), sl], up_ref[...], bias[hh],
                                jnp.zeros((tb, HEAD_DIM), F32), col < row)
        run_ref[hh] = run
        o_ref[:, sl] = pv

    def body(n, c):
        k0 = pl.multiple_of((iq - 1 - n) * tb, tb)
        for hh, sl in enumerate(heads):
            run, pv = _sb_block_ext(q[hh], k_ref[pl.ds(k0, tb), sl], v_ref[pl.ds(k0, tb), sl], up_ref[...], bias[hh],
                                    run_ref[hh], None)
            run_ref[hh] = run
            o_ref[:, sl] += pv
        return c

    lax.fori_loop(0, iq, body, 0)

    colm = lax.broadcasted_iota(jnp.int32, (tb, TAIL_ROWS), 1)
    for hh, sl in enumerate(heads):
        _, pv = _sb_block_ext(q[hh], km_ref[:, sl], vm_ref[:, sl], upm_ref[...], bias[hh], run_ref[hh],
                              colm >= META_OFF)
        o_ref[:, sl] += pv


def prompt_attention(proj, bias, seq=SEQ, n_batch=BATCH, tb=256):
    hb = HEADS_PER_STEP
    wb = hb * HEAD_DIM
    nqb = seq // tb
    qcol, kcol, vcol = SSM_WIDTH // wb, (SSM_WIDTH + ATTN_WIDTH) // wb, (SSM_WIDTH + 2 * ATTN_WIDTH) // wb
    tail_blk = (n_batch * seq) // TAIL_ROWS
    const = lambda a: pl.BlockSpec(a.shape, lambda b, h, i: (0, 0))
    up = jnp.asarray(-_upper_ext_matrix(tb), BF16)
    upm = jnp.asarray(-_upper_ext_matrix(TAIL_ROWS), BF16)
    return pl.pallas_call(
        functools.partial(_prompt_attn_kernel, tb=tb),
        out_shape=jax.ShapeDtypeStruct((n_batch * seq, ATTN_WIDTH), F32),
        grid=(n_batch, N_HEADS // hb, nqb),
        in_specs=[
            pl.BlockSpec(memory_space=pltpu.SMEM),
            pl.BlockSpec((tb, wb), lambda b, h, i: (b * nqb + i, qcol + h)),
            pl.BlockSpec((seq, wb), lambda b, h, i: (b, kcol + h)),
            pl.BlockSpec((seq, wb), lambda b, h, i: (b, vcol + h)),
            pl.BlockSpec((TAIL_ROWS, wb), lambda b, h, i: (tail_blk, kcol + h)),
            pl.BlockSpec((TAIL_ROWS, wb), lambda b, h, i: (tail_blk, vcol + h)),
            const(up), const(upm),
        ],
        out_specs=pl.BlockSpec((tb, wb), lambda b, h, i: (b * nqb + i, h)),
        scratch_shapes=[pltpu.VMEM((hb, tb, HEAD_DIM), F32)],
        compiler_params=_cparams(("parallel", "parallel", "arbitrary")),
        name="prompt_attention",
    )(bias, proj, proj, proj, proj, proj, up, upm)


def _meta_attn_kernel(bias_ref, q_ref, k_ref, v_ref, up_ref, o_ref):
    bias = bias_ref[pl.program_id(0)]
    row = lax.broadcasted_iota(jnp.int32, (TAIL_ROWS, TAIL_ROWS), 0)
    col = lax.broadcasted_iota(jnp.int32, (TAIL_ROWS, TAIL_ROWS), 1)
    mask = (col < row) & (col >= META_OFF)
    run = jnp.zeros((TAIL_ROWS, 1), F32)
    acc = jnp.zeros((TAIL_ROWS, HEAD_DIM), F32)
    _, acc = _sb_block(q_ref[...].astype(BF16), k_ref[...], v_ref[...], up_ref[...], bias, run, acc, mask)
    o_ref[...] = acc


def meta_attention(proj, bias, tail_blk=ROWS_MAIN // TAIL_ROWS):
    qcol, kcol, vcol = (SSM_WIDTH // HEAD_DIM, (SSM_WIDTH + ATTN_WIDTH) // HEAD_DIM,
                        (SSM_WIDTH + 2 * ATTN_WIDTH) // HEAD_DIM)
    blk = lambda c0: pl.BlockSpec((TAIL_ROWS, HEAD_DIM), lambda h: (tail_blk, c0 + h))
    return pl.pallas_call(
        _meta_attn_kernel,
        out_shape=jax.ShapeDtypeStruct((TAIL_ROWS, ATTN_WIDTH), F32),
        grid=(N_HEADS,),
        in_specs=[pl.BlockSpec(memory_space=pltpu.SMEM), blk(qcol), blk(kcol), blk(vcol),
                  pl.BlockSpec((TAIL_ROWS, TAIL_ROWS), lambda h: (0, 0))],
        out_specs=pl.BlockSpec((TAIL_ROWS, HEAD_DIM), lambda h: (0, h)),
        compiler_params=_cparams(("parallel",)),
        name="meta_attention",
    )(bias, proj, proj, proj, jnp.asarray(_upper_matrix(TAIL_ROWS), BF16))


PAGES_PER_STEP = 8


def _sample_attn_kernel(pt_ref, q_ref, *refs, n_steps):
    npg = PAGES_PER_STEP
    k_refs = refs[:npg]
    v_refs = refs[npg:2 * npg]
    g_ref, gt_ref, up_ref, bias_ref, o_ref, run_ref = refs[2 * npg:]
    jp = pl.program_id(1)
    nh = N_HEADS
    flat = PAGE_SIZE * nh

    @pl.when(jp == 0)
    def _():
        o_ref[...] = jnp.zeros_like(o_ref)
        run_ref[...] = jnp.zeros_like(run_ref)

    q = q_ref[...].astype(BF16)
    row = lax.broadcasted_iota(jnp.int32, (nh, flat), 0)
    col = lax.broadcasted_iota(jnp.int32, (nh, flat), 1)
    own = (col & (nh - 1)) == row

    zs = []
    for p in range(npg):
        kp = k_refs[p][...].reshape(flat, HEAD_DIM).astype(BF16)
        zt = lax.dot_general(q, kp, (((1,), (1,)), ((), ())), preferred_element_type=F32)
        zs.append(jnp.where(own, zt, 0.0))
    hi, lo = _split_bf16(jnp.concatenate(zs, axis=0))
    zc = jnp.dot(jnp.concatenate([hi, lo], axis=0), g_ref[...], preferred_element_type=F32)
    z = (zc[:npg * nh] + zc[npg * nh:]) * ATTN_SCALE + bias_ref[...]
    sp = _softplus(z)
    log_1m = -sp
    hi, lo = _split_bf16(log_1m)
    ext = jnp.dot(jnp.concatenate([hi, lo], axis=0), up_ref[...], preferred_element_type=F32)
    ext = ext[:npg * nh] + ext[npg * nh:]
    after, total = ext[:, :PAGE_SIZE], ext[:, PAGE_SIZE:]
    run = run_ref[...]
    runs = []
    for p in range(npg):
        runs.append(run)
        run = run + total[p * nh:(p + 1) * nh]
    run_ref[...] = run
    a = jnp.exp((z - sp) + after + jnp.concatenate(runs, axis=0))
    aw = jnp.dot(a.astype(BF16), gt_ref[...], preferred_element_type=F32)
    acc = o_ref[...]
    for p in range(npg):
        am = jnp.where(own, aw[p * nh:(p + 1) * nh], 0.0).astype(BF16)
        vp = v_refs[p][...].reshape(flat, HEAD_DIM).astype(BF16)
        acc = acc + jnp.dot(am, vp, preferred_element_type=F32)
    o_ref[...] = acc


def sample_attention(q, cache_k, cache_v, layer, page_table, bias):
    n_b, n_pages = page_table.shape
    npg = PAGES_PER_STEP
    n_steps = n_pages // npg
    flat = PAGE_SIZE * N_HEADS
    g = (np.arange(flat)[:, None] // N_HEADS == np.arange(PAGE_SIZE)[None, :]).astype(np.float32)
    bias_rows = jnp.tile(jnp.broadcast_to(bias[:, None], (N_HEADS, PAGE_SIZE)), (npg, 1))

    def page_spec(p):
        return pl.BlockSpec((None, None, PAGE_SIZE, N_HEADS, HEAD_DIM),
                            lambda b, jp, pt: (layer, pt[b, n_pages - 1 - (jp * npg + p)], 0, 0, 0))

    const = lambda shape: pl.BlockSpec(shape, lambda b, jp, pt: (0, 0))
    head_spec = pl.BlockSpec((None, N_HEADS, HEAD_DIM), lambda b, jp, pt: (b, 0, 0))
    grid_spec = pltpu.PrefetchScalarGridSpec(
        num_scalar_prefetch=1,
        grid=(n_b, n_steps),
        in_specs=([head_spec] + [page_spec(p) for p in range(npg)] * 2
                  + [const((flat, PAGE_SIZE)), const((PAGE_SIZE, flat)),
                     const((PAGE_SIZE, PAGE_SIZE + HEAD_DIM)), const((npg * N_HEADS, PAGE_SIZE))]),
        out_specs=head_spec,
        scratch_shapes=[pltpu.VMEM((N_HEADS, HEAD_DIM), F32)],
    )
    return pl.pallas_call(
        functools.partial(_sample_attn_kernel, n_steps=n_steps),
        out_shape=jax.ShapeDtypeStruct((n_b, N_HEADS, HEAD_DIM), F32),
        grid_spec=grid_spec,
        compiler_params=_cparams(("arbitrary", "arbitrary")),
        name="sample_attention",
    )(page_table, q, *([cache_k] * npg), *([cache_v] * npg),
      jnp.asarray(g, BF16), jnp.asarray(g.T, BF16), jnp.asarray(_upper_ext_matrix(PAGE_SIZE), BF16), bias_rows)


def _ssm_params(i, lam_re, lam_im, log_step, b_re, b_im, c_re, c_im, d, w_glu, b_glu, out_gain):
    ar, ai, bbr, bbi = ssm_discretize(lam_re[i], lam_im[i], log_step[i],
                                      jnp.swapaxes(b_re[i], 1, 2), jnp.swapaxes(b_im[i], 1, 2))
    bb = jnp.concatenate([_block_diag(bbr), _block_diag(bbi)], axis=-1).astype(BF16)
    cc = jnp.concatenate([_block_diag(jnp.swapaxes(c_re[i], 1, 2)),
                          -_block_diag(jnp.swapaxes(c_im[i], 1, 2))], axis=1).astype(BF16)
    return dict(
        ar=ar.reshape(N_GB, 1, GB_ST), ai=ai.reshape(N_GB, 1, GB_ST), bb=bb, cc=cc,
        glu=_block_diag(w_glu[i]).astype(BF16),
        d=d[i].reshape(N_GB, 1, GB_CH), bglu=b_glu[i].reshape(N_GB, 1, GB_CH),
        gain=out_gain[i].reshape(1, SSM_WIDTH))


def kernel(x_prompt, x_sample, cache_k, cache_v, state_ssm_re, state_ssm_im, page_table, meta_tokens,
           ffn1_norm_pre, ffn1_norm_post, ffn1_w_gate, ffn1_w_up, ffn1_w_down,
           mix_norm_pre, mix_norm_post, w_in, sb_logit_bias,
           ssm_lambda_re, ssm_lambda_im, ssm_log_step, ssm_b_re, ssm_b_im, ssm_c_re, ssm_c_im,
           ssm_d, ssm_w_glu, ssm_b_glu, ssm_out_norm, attn_out_norm, w_out,
           ffn2_norm_pre, ffn2_norm_post, ffn2_w_gate, ffn2_w_up, ffn2_w_down):
    n_pad = TAIL_ROWS - DEC_BATCH - N_META
    x = jnp.concatenate([x_prompt.reshape(ROWS_MAIN, D_MODEL), x_sample.reshape(DEC_BATCH, D_MODEL),
                         jnp.zeros((n_pad, D_MODEL), F32), meta_tokens.astype(F32)], axis=0)
    zero_state = jnp.zeros((SCAN_ROWS, SSM_GROUPS * SSM_STATE), F32)
    tail_blk16 = (ROWS_MAIN + META_OFF) // N_META
    kcol, vcol = SSM_WIDTH + ATTN_WIDTH, SSM_WIDTH + 2 * ATTN_WIDTH

    def ffn(x, h, i, wg, wu, wd, g_post, g_next):
        a = gateup(h, wg, wu, i)
        f = matmul(a, wd[i].astype(BF16))
        return resnorm(f, x, g_post, g_next, 0.5)

    outs = {k: [] for k in ('kp', 'vp', 'srp', 'sip', 'ks', 'vs', 'srs', 'sis')}
    h = rmsnorm_rows(x, ffn1_norm_pre[0])
    for i in range(DEPTH):
        x, h = ffn(x, h, i, ffn1_w_gate, ffn1_w_up, ffn1_w_down, ffn1_norm_post[i], mix_norm_pre[i])

        proj = matmul_wf32(h, w_in, i)
        sp = _ssm_params(i, ssm_lambda_re, ssm_lambda_im, ssm_log_step, ssm_b_re, ssm_b_im,
                         ssm_c_re, ssm_c_im, ssm_d, ssm_w_glu, ssm_b_glu, ssm_out_norm)
        ys_meta, mr, mi = ssm_mixer(proj, lambda b, ti: tail_blk16, zero_state, zero_state, sp,
                                    n_seq=BATCH, tc=N_META, nt=1, perm=True)
        ys_main, pr, pi = ssm_mixer(proj, lambda b, ti: b * (SEQ // SCAN_TC) + ti, mr, mi, sp,
                                    n_seq=BATCH, tc=SCAN_TC, nt=SEQ // SCAN_TC, perm=True)
        ys_samp, sr, si = ssm_mixer(proj, lambda b, ti: ROWS_MAIN // DEC_BATCH,
                                    state_ssm_re[i].reshape(DEC_BATCH, -1), state_ssm_im[i].reshape(DEC_BATCH, -1),
                                    sp, n_seq=DEC_BATCH, tc=1, nt=1, perm=False)
        pr, pi = pr[:BATCH], pi[:BATCH]
        ys = jnp.concatenate(list(ys_main) + [ys_samp[0], jnp.zeros((n_pad, SSM_WIDTH), BF16), ys_meta[0]], axis=0)

        at_main = prompt_attention(proj, sb_logit_bias[i])
        at_meta = meta_attention(proj, sb_logit_bias[i])
        q_samp = proj[ROWS_MAIN:ROWS_MAIN + DEC_BATCH, SSM_WIDTH:SSM_WIDTH + ATTN_WIDTH]
        at_samp = sample_attention(q_samp.reshape(DEC_BATCH, N_HEADS, HEAD_DIM), cache_k, cache_v, i, page_table,
                                   sb_logit_bias[i]).reshape(DEC_BATCH, ATTN_WIDTH)
        at = jnp.concatenate([at_main, at_samp, jnp.zeros((n_pad, ATTN_WIDTH), F32), at_meta[META_OFF:]], axis=0)
        ya = rmsnorm_rows(at, attn_out_norm[i])

        f = matmul2(ys, ya, w_out, i)
        x, h = resnorm(f, x, mix_norm_post[i], ffn2_norm_pre[i], 1.0)

        g_next = ffn1_norm_pre[i + 1] if i + 1 < DEPTH else ffn1_norm_pre[0]
        x, h = ffn(x, h, i, ffn2_w_gate, ffn2_w_up, ffn2_w_down, ffn2_norm_post[i], g_next)

        k_all, v_all = proj[:, kcol:kcol + ATTN_WIDTH], proj[:, vcol:vcol + ATTN_WIDTH]

        def prompt_rows(t):
            meta = jnp.broadcast_to(t[ROWS_MAIN + META_OFF:][None], (BATCH, N_META, ATTN_WIDTH))
            full = jnp.concatenate([meta, t[:ROWS_MAIN].reshape(BATCH, SEQ, ATTN_WIDTH)], axis=1)
            return full.reshape(BATCH, N_META + SEQ, N_HEADS, HEAD_DIM)

        sample_rows = lambda t: t[ROWS_MAIN:ROWS_MAIN + DEC_BATCH].reshape(DEC_BATCH, 1, N_HEADS, HEAD_DIM)
        outs['kp'].append(prompt_rows(k_all))
        outs['vp'].append(prompt_rows(v_all))
        outs['ks'].append(sample_rows(k_all))
        outs['vs'].append(sample_rows(v_all))
        outs['srp'].append(pr.reshape(BATCH, SSM_GROUPS, SSM_STATE))
        outs['sip'].append(pi.reshape(BATCH, SSM_GROUPS, SSM_STATE))
        outs['srs'].append(sr.reshape(DEC_BATCH, SSM_GROUPS, SSM_STATE))
        outs['sis'].append(si.reshape(DEC_BATCH, SSM_GROUPS, SSM_STATE))

    y_prompt = x[:ROWS_MAIN].reshape(BATCH, SEQ, D_MODEL)
    y_sample = x[ROWS_MAIN:ROWS_MAIN + DEC_BATCH].reshape(DEC_BATCH, 1, D_MODEL)
    st = lambda k: jnp.stack(outs[k])
    return (y_prompt, y_sample, st('kp'), st('vp'), st('srp'), st('sip'),
            st('ks'), st('vs'), st('srs'), st('sis'))
```

```python
import functools
import math

import numpy as np
import jax
import jax.numpy as jnp
from jax import lax
from jax.experimental import pallas as pl
from jax.experimental.pallas import tpu as pltpu

F32 = jnp.float32
BF16 = jnp.bfloat16

D_MODEL = 4096
BATCH = 4
SEQ = 2048
DEPTH = 2
DEC_BATCH = 8
PAGE_SIZE = 128
N_HEADS = 16
HEAD_DIM = 128
ATTN_WIDTH = N_HEADS * HEAD_DIM
SSM_CH = 16
SSM_GROUPS = 128
SSM_STATE = 64
SSM_WIDTH = SSM_GROUPS * SSM_CH
D_FF = 11008
N_META = 16
RMS_EPS = 1e-6
LAMBDA_RE_MAX = -1e-4
ATTN_SCALE = 1.0 / math.sqrt(HEAD_DIM)

ROWS_MAIN = BATCH * SEQ
TAIL_ROWS = 128
ROWS = ROWS_MAIN + TAIL_ROWS
META_OFF = TAIL_ROWS - N_META

GB = 16
N_GB = SSM_GROUPS // GB
GB_CH = GB * SSM_CH
GB_ST = GB * SSM_STATE

TM = 640
VMEM_LIMIT = 56 * 1024 * 1024


def _cparams(sem, vmem=VMEM_LIMIT):
    return pltpu.CompilerParams(dimension_semantics=sem, vmem_limit_bytes=vmem)


def _rms(x, gain):
    r = lax.rsqrt(jnp.mean(x * x, axis=-1, keepdims=True) + RMS_EPS)
    return x * r * gain


def _rows_of(main_ref, tail_ref):
    n_main = pl.num_programs(0) - 1
    return jnp.where(pl.program_id(0) < n_main, main_ref[...], tail_ref[...])


def _split_row_specs(main, tm):
    d = main.shape[1]
    last = main.shape[0] // tm - 1
    return [pl.BlockSpec((tm, d), lambda i: (jnp.minimum(i, last), 0)), pl.BlockSpec((tm, d), lambda i: (0, 0))]


def _rmsnorm_kernel(xm_ref, xt_ref, g_ref, o_ref):
    o_ref[...] = _rms(_rows_of(xm_ref, xt_ref), g_ref[...]).astype(o_ref.dtype)


def rmsnorm_rows(x_main, x_tail, gain, out_dtype=BF16):
    tm = TAIL_ROWS
    rows, d = x_main.shape[0] + tm, x_main.shape[1]
    return pl.pallas_call(
        _rmsnorm_kernel,
        out_shape=jax.ShapeDtypeStruct((rows, d), out_dtype),
        grid=(rows // tm,),
        in_specs=_split_row_specs(x_main, tm) + [pl.BlockSpec((1, d), lambda i: (0, 0))],
        out_specs=pl.BlockSpec((tm, d), lambda i: (i, 0)),
        compiler_params=_cparams(("arbitrary",)),
        name="rmsnorm_rows",
    )(x_main, x_tail, gain.reshape(1, d))


def _gateup_kernel(h_ref, wg_ref, wu_ref, wd_ref, o_ref, wdb_ref, wgb_ref, wub_ref):
    @pl.when(pl.program_id(1) == 0)
    def _():
        wgb_ref[...] = wg_ref[...].astype(BF16)
        wub_ref[...] = wu_ref[...].astype(BF16)
        wdb_ref[...] = wd_ref[...].astype(BF16)

    h = h_ref[...]
    g = jnp.dot(h, wgb_ref[...], preferred_element_type=F32)
    u = jnp.dot(h, wub_ref[...], preferred_element_type=F32)
    o_ref[...] = (g * jax.nn.sigmoid(g) * u).astype(o_ref.dtype)


def gateup(h, wg, wu, wd, layer, tm=TM, tf=256):
    rows, d = h.shape
    f = wg.shape[2]
    w_spec = pl.BlockSpec((None, d, tf), lambda j, i: (layer, 0, j))
    return pl.pallas_call(
        _gateup_kernel,
        out_shape=(jax.ShapeDtypeStruct((rows, f), BF16), jax.ShapeDtypeStruct((f, d), BF16)),
        grid=(f // tf, rows // tm),
        in_specs=[pl.BlockSpec((tm, d), lambda j, i: (i, 0)), w_spec, w_spec,
                  pl.BlockSpec((None, tf, d), lambda j, i: (layer, j, 0))],
        out_specs=(pl.BlockSpec((tm, tf), lambda j, i: (i, j)), pl.BlockSpec((tf, d), lambda j, i: (j, 0))),
        scratch_shapes=[pltpu.VMEM((d, tf), BF16), pltpu.VMEM((d, tf), BF16)],
        compiler_params=_cparams(("arbitrary", "arbitrary")),
        name="ffn_gateup",
    )(h, wg, wu, wd)


def _mm_kernel(a_ref, w_ref, o_ref):
    o_ref[...] = jnp.dot(a_ref[...], w_ref[...], preferred_element_type=F32).astype(o_ref.dtype)


def matmul(a, w, tm=TM, tn=256, out_dtype=F32):
    rows, k = a.shape
    n = w.shape[1]
    return pl.pallas_call(
        _mm_kernel,
        out_shape=jax.ShapeDtypeStruct((rows, n), out_dtype),
        grid=(rows // tm, n // tn),
        in_specs=[pl.BlockSpec((tm, k), lambda i, j: (i, 0)),
                  pl.BlockSpec((k, tn), lambda i, j: (0, j))],
        out_specs=pl.BlockSpec((tm, tn), lambda i, j: (i, j)),
        compiler_params=_cparams(("parallel", "arbitrary")),
        name="matmul",
    )(a, w)


def _mm_wcast_kernel(a_ref, w_ref, o_ref, wb_ref):
    @pl.when(pl.program_id(1) == 0)
    def _():
        wb_ref[...] = w_ref[...].astype(BF16)

    o_ref[...] = jnp.dot(a_ref[...], wb_ref[...], preferred_element_type=F32)


def matmul_wf32(a, w, layer, tm=TM, tn=512):
    rows, k = a.shape
    n = w.shape[2]
    return pl.pallas_call(
        _mm_wcast_kernel,
        out_shape=jax.ShapeDtypeStruct((rows, n), F32),
        grid=(n // tn, rows // tm),
        in_specs=[pl.BlockSpec((tm, k), lambda j, i: (i, 0)),
                  pl.BlockSpec((None, k, tn), lambda j, i: (layer, 0, j))],
        out_specs=pl.BlockSpec((tm, tn), lambda j, i: (i, j)),
        scratch_shapes=[pltpu.VMEM((k, tn), BF16)],
        compiler_params=_cparams(("arbitrary", "arbitrary")),
        name="matmul_wf32",
    )(a, w)


def _mm2_kernel(a1_ref, a2_ref, w1_ref, w2_ref, o_ref, w1b_ref, w2b_ref):
    @pl.when(pl.program_id(1) == 0)
    def _():
        w1b_ref[...] = w1_ref[...].astype(BF16)
        w2b_ref[...] = w2_ref[...].astype(BF16)

    acc = jnp.dot(a1_ref[...], w1b_ref[...], preferred_element_type=F32)
    acc = acc + jnp.dot(a2_ref[...], w2b_ref[...], preferred_element_type=F32)
    o_ref[...] = acc


def matmul2(a1, a2, w, layer, tm=TM, tn=512):
    rows, k1 = a1.shape
    k2 = a2.shape[1]
    assert k1 == k2
    n = w.shape[2]
    return pl.pallas_call(
        _mm2_kernel,
        out_shape=jax.ShapeDtypeStruct((rows, n), F32),
        grid=(n // tn, rows // tm),
        in_specs=[pl.BlockSpec((tm, k1), lambda j, i: (i, 0)),
                  pl.BlockSpec((tm, k2), lambda j, i: (i, 0)),
                  pl.BlockSpec((None, k1, tn), lambda j, i: (layer, 0, j)),
                  pl.BlockSpec((None, k2, tn), lambda j, i: (layer, 1, j))],
        out_specs=pl.BlockSpec((tm, tn), lambda j, i: (i, j)),
        scratch_shapes=[pltpu.VMEM((k1, tn), BF16), pltpu.VMEM((k2, tn), BF16)],
        compiler_params=_cparams(("arbitrary", "arbitrary")),
        name="matmul2",
    )(a1, a2, w, w)


def _resnorm_kernel(f_ref, *refs, scale, split):
    if split:
        xm_ref, xt_ref, gp_ref, gn_ref, xo_ref, ho_ref = refs
        x = _rows_of(xm_ref, xt_ref)
    else:
        x_ref, gp_ref, gn_ref, xo_ref, ho_ref = refs
        x = x_ref[...]
    xn = x + scale * _rms(f_ref[...], gp_ref[...])
    xo_ref[...] = xn
    ho_ref[...] = _rms(xn, gn_ref[...]).astype(ho_ref.dtype)


def resnorm(f, x, gain_post, gain_next, scale):
    tm = TAIL_ROWS
    rows, d = f.shape
    row_spec = pl.BlockSpec((tm, d), lambda i: (i, 0))
    vec_spec = pl.BlockSpec((1, d), lambda i: (0, 0))
    split = isinstance(x, tuple)
    x_specs, x_args = (_split_row_specs(x[0], tm), list(x)) if split else ([row_spec], [x])
    return pl.pallas_call(
        functools.partial(_resnorm_kernel, scale=scale, split=split),
        out_shape=(jax.ShapeDtypeStruct((rows, d), F32), jax.ShapeDtypeStruct((rows, d), BF16)),
        grid=(rows // tm,),
        in_specs=[row_spec] + x_specs + [vec_spec, vec_spec],
        out_specs=(row_spec, row_spec),
        compiler_params=_cparams(("arbitrary",)),
        name="resnorm",
    )(f, *x_args, gain_post.reshape(1, d), gain_next.reshape(1, d))


def _resnorm_last_kernel(f_ref, x_ref, gp_ref, xo_ref, *, scale):
    xo_ref[...] = x_ref[...] + scale * _rms(f_ref[...], gp_ref[...])


def resnorm_last(f, x, gain_post, scale, row0, n_rows, tm=128):
    d = x.shape[1]
    b0 = row0 // tm
    row_spec = pl.BlockSpec((tm, d), lambda i: (b0 + i, 0))
    return pl.pallas_call(
        functools.partial(_resnorm_last_kernel, scale=scale),
        out_shape=jax.ShapeDtypeStruct((n_rows, d), F32),
        grid=(n_rows // tm,),
        in_specs=[row_spec, row_spec, pl.BlockSpec((1, d), lambda i: (0, 0))],
        out_specs=pl.BlockSpec((tm, d), lambda i: (i, 0)),
        compiler_params=_cparams(("parallel",)),
        name="resnorm_last",
    )(f, x, gain_post.reshape(1, d))


def _ssm_discretize_kernel(lre_ref, lim_ref, ls_ref, bre_ref, bim_ref,
                           ar_ref, ai_ref, bbr_ref, bbi_ref):
    step = jnp.exp(ls_ref[...])
    lr = jnp.minimum(lre_ref[...], LAMBDA_RE_MAX)
    li = lim_ref[...]
    mag = jnp.exp(lr * step)
    ar = mag * jnp.cos(li * step)
    ai = mag * jnp.sin(li * step)
    den = lr * lr + li * li
    fr = ((ar - 1.0) * lr + ai * li) / den
    fi = (ai * lr - (ar - 1.0) * li) / den
    br = bre_ref[...]
    bi = bim_ref[...]
    ar_ref[...] = ar
    ai_ref[...] = ai
    bbr_ref[...] = fr * br - fi * bi
    bbi_ref[...] = fr * bi + fi * br


def ssm_discretize(lam_re, lam_im, log_step, b_re, b_im):
    g, p = lam_re.shape
    ch = b_re.shape[1]
    gp = jax.ShapeDtypeStruct((g, 1, p), F32)
    gcp = jax.ShapeDtypeStruct((g, ch, p), F32)
    return pl.pallas_call(
        _ssm_discretize_kernel,
        out_shape=(gp, gp, gcp, gcp),
        name="ssm_discretize",
    )(lam_re.reshape(g, 1, p), lam_im.reshape(g, 1, p), log_step.reshape(g, 1, 1), b_re, b_im)


def _block_diag(x):
    _, r, c = x.shape
    x = x.reshape(N_GB, GB, r, c)
    eye = jnp.eye(GB, dtype=x.dtype)
    return jnp.einsum('bgrc,gh->bgrhc', x, eye).reshape(N_GB, GB * r, GB * c)


SCAN_ROWS = 8
SCAN_TC = 64
GB_PER_STEP = 2


def _perm_matrix(n_seq, tc):
    p = np.zeros((SCAN_ROWS * tc, n_seq * tc), np.float32)
    t, b = np.meshgrid(np.arange(tc), np.arange(n_seq), indexing='ij')
    p[(t * SCAN_ROWS + b).ravel(), (b * tc + t).ravel()] = 1.0
    return p


def _ssm_kernel(*refs, n_seq, tc, perm):
    nb = SCAN_ROWS
    n_u = n_seq if perm else 1
    u_refs = refs[:n_u]
    i = n_u
    if perm:
        p_ref, pt_ref = refs[i], refs[i + 1]
        i += 2
    (s0r_ref, s0i_ref, ar_ref, ai_ref, bb_ref, cc_ref, glu_ref, d_ref, bglu_ref, gain_ref) = refs[i:i + 10]
    i += 10
    y_refs = refs[i:i + n_u]
    i += n_u
    str_ref, sti_ref = refs[i], refs[i + 1]
    uhi_ref, ulo_ref, bu_ref, ss_ref, yacc_ref, carry_ref = refs[i + 2:]

    ti = pl.program_id(0)
    gi = pl.program_id(1)

    @pl.when(gi == 0)
    def _():
        if perm:
            ucat = jnp.concatenate([r[...] for r in u_refs], axis=0)
        else:
            ucat = u_refs[0][...]
        hi = ucat.astype(BF16)
        lo = (ucat - hi.astype(F32)).astype(BF16)
        if perm:
            hi = jnp.dot(p_ref[...], hi, preferred_element_type=F32).astype(BF16)
            lo = jnp.dot(p_ref[...], lo, preferred_element_type=F32).astype(BF16)
        for k in range(N_GB):
            uhi_ref[k] = hi[:, k * GB_CH:(k + 1) * GB_CH]
            ulo_ref[k] = lo[:, k * GB_CH:(k + 1) * GB_CH]

    @pl.when(ti == 0)
    def _():
        for k in range(GB_PER_STEP):
            sl = slice(k * GB_ST, (k + 1) * GB_ST)
            carry_ref[gi * GB_PER_STEP + k] = jnp.concatenate([s0r_ref[:, sl], s0i_ref[:, sl]], axis=-1)

    for k in range(GB_PER_STEP):
        bu_ref[k] = jnp.dot(uhi_ref[gi * GB_PER_STEP + k], bb_ref[k], preferred_element_type=F32)

    for k in range(GB_PER_STEP):
        be = gi * GB_PER_STEP + k
        a_re = jnp.broadcast_to(ar_ref[k], (nb, GB_ST))
        a_im = jnp.broadcast_to(ai_ref[k], (nb, GB_ST))
        s0 = carry_ref[be]
        sr, si = s0[:, :GB_ST], s0[:, GB_ST:]
        for t in range(tc):
            b = bu_ref[k, t * nb:(t + 1) * nb, :]
            sr, si = a_re * sr - a_im * si + b[:, :GB_ST], a_re * si + a_im * sr + b[:, GB_ST:]
            ss_ref[k, t * nb:(t + 1) * nb, :] = jnp.concatenate([sr, si], axis=-1)
        carry_ref[be] = jnp.concatenate([sr, si], axis=-1)
        str_ref[be] = sr
        sti_ref[be] = si

        y = jnp.dot(ss_ref[k].astype(BF16), cc_ref[k], preferred_element_type=F32)
        u32 = uhi_ref[be].astype(F32) + ulo_ref[be].astype(F32)
        y = y + d_ref[k] * u32
        g = jax.nn.gelu(y)
        gate = jnp.dot(g.astype(BF16), glu_ref[k], preferred_element_type=F32) + bglu_ref[k]
        yacc_ref[be] = g * jax.nn.sigmoid(gate)

    @pl.when(gi == N_GB // GB_PER_STEP - 1)
    def _():
        o = jnp.concatenate([yacc_ref[k] for k in range(N_GB)], axis=-1)
        on = _rms(o, gain_ref[...]).astype(BF16)
        if perm:
            on = jnp.dot(pt_ref[...], on, preferred_element_type=F32).astype(BF16)
            for b in range(n_seq):
                y_refs[b][...] = on[b * tc:(b + 1) * tc]
        else:
            y_refs[0][...] = on


def ssm_mixer(u_src, u_row_block, s0_re, s0_im, sp, *, n_seq, tc, nt, perm):
    nb = SCAN_ROWS
    rows = nb * tc
    n_u = n_seq if perm else 1
    blk_rows = tc if perm else rows
    in_specs = [pl.BlockSpec((blk_rows, SSM_WIDTH), functools.partial(lambda ti, be, b: (u_row_block(b, ti), 0), b=b))
                for b in range(n_u)]
    args = [u_src] * n_u
    if perm:
        pm = _perm_matrix(n_seq, tc)
        in_specs += [pl.BlockSpec(pm.shape, lambda ti, be: (0, 0)), pl.BlockSpec(pm.T.shape, lambda ti, be: (0, 0))]
        args += [jnp.asarray(pm, BF16), jnp.asarray(pm.T, BF16)]
    gps = GB_PER_STEP
    st_spec = pl.BlockSpec((nb, gps * GB_ST), lambda ti, gi: (0, gi))
    per_block = lambda r, c: pl.BlockSpec((gps, r, c), lambda ti, gi: (gi, 0, 0))
    in_specs += [
        st_spec, st_spec,
        per_block(1, GB_ST), per_block(1, GB_ST),
        per_block(GB_CH, 2 * GB_ST), per_block(2 * GB_ST, GB_CH), per_block(GB_CH, GB_CH),
        per_block(1, GB_CH), per_block(1, GB_CH),
        pl.BlockSpec((1, SSM_WIDTH), lambda ti, gi: (0, 0)),
    ]
    args += [s0_re, s0_im, sp['ar'], sp['ai'], sp['bb'], sp['cc'], sp['glu'], sp['d'], sp['bglu'], sp['gain']]
    y_shape = jax.ShapeDtypeStruct((nt * blk_rows, SSM_WIDTH), BF16)
    st_shape = jax.ShapeDtypeStruct((N_GB, nb, GB_ST), F32)
    st_out_spec = pl.BlockSpec((N_GB, nb, GB_ST), lambda ti, be: (0, 0, 0))
    out = pl.pallas_call(
        functools.partial(_ssm_kernel, n_seq=n_seq, tc=tc, perm=perm),
        out_shape=tuple([y_shape] * n_u + [st_shape, st_shape]),
        grid=(nt, N_GB // gps),
        in_specs=in_specs,
        out_specs=tuple([pl.BlockSpec((blk_rows, SSM_WIDTH), lambda ti, be: (ti, 0))] * n_u + [st_out_spec, st_out_spec]),
        scratch_shapes=[
            pltpu.VMEM((N_GB, rows, GB_CH), BF16),
            pltpu.VMEM((N_GB, rows, GB_CH), BF16),
            pltpu.VMEM((gps, rows, 2 * GB_ST), F32),
            pltpu.VMEM((gps, rows, 2 * GB_ST), F32),
            pltpu.VMEM((N_GB, rows, GB_CH), F32),
            pltpu.VMEM((N_GB, nb, 2 * GB_ST), F32),
        ],
        compiler_params=_cparams(("arbitrary", "arbitrary")),
        name="ssm_mixer",
    )(*args)
    flat = lambda s: jnp.swapaxes(s, 0, 1).reshape(nb, N_GB * GB_ST)
    return out[:n_u], flat(out[n_u]), flat(out[n_u + 1])


def _softplus(z):
    return jnp.maximum(z, 0.0) + jnp.log(1.0 + jnp.exp(-jnp.abs(z)))


def _split_bf16(x):
    hi = x.astype(BF16)
    return hi, (x - hi.astype(F32)).astype(BF16)


def _sb_block(q, kb, vb, upper, bias, run, acc, mask):
    z = lax.dot_general(q, kb.astype(BF16), (((1,), (1,)), ((), ())), preferred_element_type=F32)
    z = z * ATTN_SCALE + bias
    sp = _softplus(z)
    log_1m = -sp
    if mask is not None:
        log_1m = jnp.where(mask, log_1m, 0.0)
    hi, lo = _split_bf16(log_1m)
    after = (jnp.dot(hi, upper, preferred_element_type=F32)
             + jnp.dot(lo, upper, preferred_element_type=F32))
    a = jnp.exp((z - sp) + after + run)
    if mask is not None:
        a = jnp.where(mask, a, 0.0)
    acc = acc + jnp.dot(a.astype(BF16), vb.astype(BF16), preferred_element_type=F32)
    run = run + jnp.sum(log_1m, axis=1, keepdims=True)
    return run, acc


def _upper_matrix(n):
    j, s = np.meshgrid(np.arange(n), np.arange(n), indexing='ij')
    return (j > s).astype(np.float32)


def _upper_ext_matrix(n):
    return np.concatenate([_upper_matrix(n), np.ones((n, HEAD_DIM), np.float32)], axis=1)


LOG2E = 1.4426950408889634


def _sb_block_ext(q2, kb, vb, neg_upper_ext, bias2, run, mask):
    tk = kb.shape[0]
    w = lax.dot_general(q2, kb.astype(BF16), (((1,), (1,)), ((), ())), preferred_element_type=F32) + bias2
    sp2 = jnp.maximum(w, 0.0) + jnp.log2(1.0 + jnp.exp2(-jnp.abs(w)))
    if mask is not None:
        sp2 = jnp.where(mask, sp2, 0.0)
    ext = jnp.dot(sp2.astype(BF16), neg_upper_ext, preferred_element_type=F32)
    run_wide = jnp.concatenate([run] * (tk // HEAD_DIM), axis=1)
    a = jnp.exp2((w - sp2) + ext[:, :tk] + run_wide)
    if mask is not None:
        a = jnp.where(mask, a, 0.0)
    pv = jnp.dot(a.astype(BF16), vb.astype(BF16), preferred_element_type=F32)
    return run + ext[:, tk:], pv


HEADS_PER_STEP = 8


def _prompt_attn_kernel(bias_ref, q_ref, k_ref, v_ref, km_ref, vm_ref, up_ref, upm_ref, o_ref, run_ref, *, tb):
    hb = HEADS_PER_STEP
    hg = pl.program_id(1)
    iq = pl.program_id(2)
    heads = [slice(hh * HEAD_DIM, (hh + 1) * HEAD_DIM) for hh in range(hb)]
    bias = [bias_ref[hg * hb + hh] * LOG2E for hh in range(hb)]
    q = [(q_ref[:, sl] * (ATTN_SCALE * LOG2E)).astype(BF16) for sl in heads]
    row = lax.broadcasted_iota(jnp.int32, (tb, tb), 0)
    col = lax.broadcasted_iota(jnp.int32, (tb, tb), 1)

    d0 = pl.multiple_of(iq * tb, tb)
    for hh, sl in enumerate(heads):
        run, pv = _sb_block_ext(q[hh], k_ref[pl.ds(d0, tb), sl], v_ref[pl.ds(d0, tb), sl], up_ref[...], bias[hh],
                                jnp.zeros((tb, HEAD_DIM), F32), col < row)
        run_ref[hh] = run
        o_ref[:, sl] = pv

    def body(n, c):
        k0 = pl.multiple_of((iq - 1 - n) * tb, tb)
        for hh, sl in enumerate(heads):
            run, pv = _sb_block_ext(q[hh], k_ref[pl.ds(k0, tb), sl], v_ref[pl.ds(k0, tb), sl], up_ref[...], bias[hh],
                                    run_ref[hh], None)
            run_ref[hh] = run
            o_ref[:, sl] += pv
        return c

    lax.fori_loop(0, iq, body, 0)

    colm = lax.broadcasted_iota(jnp.int32, (tb, TAIL_ROWS), 1)
    for hh, sl in enumerate(heads):
        _, pv = _sb_block_ext(q[hh], km_ref[:, sl], vm_ref[:, sl], upm_ref[...], bias[hh], run_ref[hh],
                              colm >= META_OFF)
        o_ref[:, sl] += pv


def prompt_attention(proj, bias, seq=SEQ, n_batch=BATCH, tb=256):
    hb = HEADS_PER_STEP
    wb = hb * HEAD_DIM
    nqb = seq // tb
    qcol, kcol, vcol = SSM_WIDTH // wb, (SSM_WIDTH + ATTN_WIDTH) // wb, (SSM_WIDTH + 2 * ATTN_WIDTH) // wb
    tail_blk = (n_batch * seq) // TAIL_ROWS
    const = lambda a: pl.BlockSpec(a.shape, lambda b, h, i: (0, 0))
    up = jnp.asarray(-_upper_ext_matrix(tb), BF16)
    upm = jnp.asarray(-_upper_ext_matrix(TAIL_ROWS), BF16)
    return pl.pallas_call(
        functools.partial(_prompt_attn_kernel, tb=tb),
        out_shape=jax.ShapeDtypeStruct((n_batch * seq, ATTN_WIDTH), F32),
        grid=(n_batch, N_HEADS // hb, nqb),
        in_specs=[
            pl.BlockSpec(memory_space=pltpu.SMEM),
            pl.BlockSpec((tb, wb), lambda b, h, i: (b * nqb + i, qcol + h)),
            pl.BlockSpec((seq, wb), lambda b, h, i: (b, kcol + h)),
            pl.BlockSpec((seq, wb), lambda b, h, i: (b, vcol + h)),
            pl.BlockSpec((TAIL_ROWS, wb), lambda b, h, i: (tail_blk, kcol + h)),
            pl.BlockSpec((TAIL_ROWS, wb), lambda b, h, i: (tail_blk, vcol + h)),
            const(up), const(upm),
        ],
        out_specs=pl.BlockSpec((tb, wb), lambda b, h, i: (b * nqb + i, h)),
        scratch_shapes=[pltpu.VMEM((hb, tb, HEAD_DIM), F32)],
        compiler_params=_cparams(("parallel", "parallel", "arbitrary")),
        name="prompt_attention",
    )(bias, proj, proj, proj, proj, proj, up, upm)


def _meta_attn_kernel(bias_ref, q_ref, k_ref, v_ref, up_ref, o_ref):
    bias = bias_ref[pl.program_id(0)]
    row = lax.broadcasted_iota(jnp.int32, (TAIL_ROWS, TAIL_ROWS), 0)
    col = lax.broadcasted_iota(jnp.int32, (TAIL_ROWS, TAIL_ROWS), 1)
    mask = (col < row) & (col >= META_OFF)
    run = jnp.zeros((TAIL_ROWS, 1), F32)
    acc = jnp.zeros((TAIL_ROWS, HEAD_DIM), F32)
    _, acc = _sb_block(q_ref[...].astype(BF16), k_ref[...], v_ref[...], up_ref[...], bias, run, acc, mask)
    o_ref[...] = acc


def meta_attention(proj, bias, tail_blk=ROWS_MAIN // TAIL_ROWS):
    qcol, kcol, vcol = (SSM_WIDTH // HEAD_DIM, (SSM_WIDTH + ATTN_WIDTH) // HEAD_DIM,
                        (SSM_WIDTH + 2 * ATTN_WIDTH) // HEAD_DIM)
    blk = lambda c0: pl.BlockSpec((TAIL_ROWS, HEAD_DIM), lambda h: (tail_blk, c0 + h))
    return pl.pallas_call(
        _meta_attn_kernel,
        out_shape=jax.ShapeDtypeStruct((TAIL_ROWS, ATTN_WIDTH), F32),
        grid=(N_HEADS,),
        in_specs=[pl.BlockSpec(memory_space=pltpu.SMEM), blk(qcol), blk(kcol), blk(vcol),
                  pl.BlockSpec((TAIL_ROWS, TAIL_ROWS), lambda h: (0, 0))],
        out_specs=pl.BlockSpec((TAIL_ROWS, HEAD_DIM), lambda h: (0, h)),
        compiler_params=_cparams(("parallel",)),
        name="meta_attention",
    )(bias, proj, proj, proj, jnp.asarray(_upper_matrix(TAIL_ROWS), BF16))


PAGES_PER_STEP = 8


def _sample_attn_kernel(pt_ref, q_ref, *refs, n_steps):
    npg = PAGES_PER_STEP
    k_refs = refs[:npg]
    v_refs = refs[npg:2 * npg]
    g_ref, gt_ref, up_ref, bias_ref, o_ref, run_ref = refs[2 * npg:]
    jp = pl.program_id(1)
    nh = N_HEADS
    flat = PAGE_SIZE * nh

    @pl.when(jp == 0)
    def _():
        o_ref[...] = jnp.zeros_like(o_ref)
        run_ref[...] = jnp.zeros_like(run_ref)

    q = q_ref[...].astype(BF16)
    row = lax.broadcasted_iota(jnp.int32, (nh, flat), 0)
    col = lax.broadcasted_iota(jnp.int32, (nh, flat), 1)
    own = (col & (nh - 1)) == row

    zs = []
    for p in range(npg):
        kp = k_refs[p][...].reshape(flat, HEAD_DIM).astype(BF16)
        zt = lax.dot_general(q, kp, (((1,), (1,)), ((), ())), preferred_element_type=F32)
        zs.append(jnp.where(own, zt, 0.0))
    hi, lo = _split_bf16(jnp.concatenate(zs, axis=0))
    zc = jnp.dot(jnp.concatenate([hi, lo], axis=0), g_ref[...], preferred_element_type=F32)
    z = (zc[:npg * nh] + zc[npg * nh:]) * ATTN_SCALE + bias_ref[...]
    sp = _softplus(z)
    log_1m = -sp
    hi, lo = _split_bf16(log_1m)
    ext = jnp.dot(jnp.concatenate([hi, lo], axis=0), up_ref[...], preferred_element_type=F32)
    ext = ext[:npg * nh] + ext[npg * nh:]
    after, total = ext[:, :PAGE_SIZE], ext[:, PAGE_SIZE:]
    run = run_ref[...]
    runs = []
    for p in range(npg):
        runs.append(run)
        run = run + total[p * nh:(p + 1) * nh]
    run_ref[...] = run
    a = jnp.exp((z - sp) + after + jnp.concatenate(runs, axis=0))
    aw = jnp.dot(a.astype(BF16), gt_ref[...], preferred_element_type=F32)
    acc = o_ref[...]
    for p in range(npg):
        am = jnp.where(own, aw[p * nh:(p + 1) * nh], 0.0).astype(BF16)
        vp = v_refs[p][...].reshape(flat, HEAD_DIM).astype(BF16)
        acc = acc + jnp.dot(am, vp, preferred_element_type=F32)
    o_ref[...] = acc


def sample_attention(q, cache_k, cache_v, layer, page_table, bias):
    n_b, n_pages = page_table.shape
    npg = PAGES_PER_STEP
    n_steps = n_pages // npg
    flat = PAGE_SIZE * N_HEADS
    g = (np.arange(flat)[:, None] // N_HEADS == np.arange(PAGE_SIZE)[None, :]).astype(np.float32)
    bias_rows = jnp.tile(jnp.broadcast_to(bias[:, None], (N_HEADS, PAGE_SIZE)), (npg, 1))

    def page_spec(p):
        return pl.BlockSpec((None, None, PAGE_SIZE, N_HEADS, HEAD_DIM),
                            lambda b, jp, pt: (layer, pt[b, n_pages - 1 - (jp * npg + p)], 0, 0, 0))

    const = lambda shape: pl.BlockSpec(shape, lambda b, jp, pt: (0, 0))
    head_spec = pl.BlockSpec((None, N_HEADS, HEAD_DIM), lambda b, jp, pt: (b, 0, 0))
    grid_spec = pltpu.PrefetchScalarGridSpec(
        num_scalar_prefetch=1,
        grid=(n_b, n_steps),
        in_specs=([head_spec] + [page_spec(p) for p in range(npg)] * 2
                  + [const((flat, PAGE_SIZE)), const((PAGE_SIZE, flat)),
                     const((PAGE_SIZE, PAGE_SIZE + HEAD_DIM)), const((npg * N_HEADS, PAGE_SIZE))]),
        out_specs=head_spec,
        scratch_shapes=[pltpu.VMEM((N_HEADS, HEAD_DIM), F32)],
    )
    return pl.pallas_call(
        functools.partial(_sample_attn_kernel, n_steps=n_steps),
        out_shape=jax.ShapeDtypeStruct((n_b, N_HEADS, HEAD_DIM), F32),
        grid_spec=grid_spec,
        compiler_params=_cparams(("arbitrary", "arbitrary")),
        name="sample_attention",
    )(page_table, q, *([cache_k] * npg), *([cache_v] * npg),
      jnp.asarray(g, BF16), jnp.asarray(g.T, BF16), jnp.asarray(_upper_ext_matrix(PAGE_SIZE), BF16), bias_rows)


def _ssm_params(i, lam_re, lam_im, log_step, b_re, b_im, c_re, c_im, d, w_glu, b_glu, out_gain):
    ar, ai, bbr, bbi = ssm_discretize(lam_re[i], lam_im[i], log_step[i],
                                      jnp.swapaxes(b_re[i], 1, 2), jnp.swapaxes(b_im[i], 1, 2))
    bb = jnp.concatenate([_block_diag(bbr), _block_diag(bbi)], axis=-1).astype(BF16)
    cc = jnp.concatenate([_block_diag(jnp.swapaxes(c_re[i], 1, 2)),
                          -_block_diag(jnp.swapaxes(c_im[i], 1, 2))], axis=1).astype(BF16)
    return dict(
        ar=ar.reshape(N_GB, 1, GB_ST), ai=ai.reshape(N_GB, 1, GB_ST), bb=bb, cc=cc,
        glu=_block_diag(w_glu[i]).astype(BF16),
        d=d[i].reshape(N_GB, 1, GB_CH), bglu=b_glu[i].reshape(N_GB, 1, GB_CH),
        gain=out_gain[i].reshape(1, SSM_WIDTH))


def kernel(x_prompt, x_sample, cache_k, cache_v, state_ssm_re, state_ssm_im, page_table, meta_tokens,
           ffn1_norm_pre, ffn1_norm_post, ffn1_w_gate, ffn1_w_up, ffn1_w_down,
           mix_norm_pre, mix_norm_post, w_in, sb_logit_bias,
           ssm_lambda_re, ssm_lambda_im, ssm_log_step, ssm_b_re, ssm_b_im, ssm_c_re, ssm_c_im,
           ssm_d, ssm_w_glu, ssm_b_glu, ssm_out_norm, attn_out_norm, w_out,
           ffn2_norm_pre, ffn2_norm_post, ffn2_w_gate, ffn2_w_up, ffn2_w_down):
    n_pad = TAIL_ROWS - DEC_BATCH - N_META
    tail_of = lambda sample, meta: jnp.concatenate([sample, jnp.zeros((n_pad, sample.shape[1]), sample.dtype), meta], 0)
    x = (x_prompt.reshape(ROWS_MAIN, D_MODEL), tail_of(x_sample.reshape(DEC_BATCH, D_MODEL), meta_tokens.astype(F32)))
    zero_state = jnp.zeros((SCAN_ROWS, SSM_GROUPS * SSM_STATE), F32)
    tail_blk16 = (ROWS_MAIN + META_OFF) // N_META
    kcol, vcol = SSM_WIDTH + ATTN_WIDTH, SSM_WIDTH + 2 * ATTN_WIDTH

    def ffn(x, h, i, wg, wu, wd, g_post, g_next):
        a, wd_bf16 = gateup(h, wg, wu, wd, i)
        f = matmul(a, wd_bf16)
        if g_next is None:
            return (resnorm_last(f, x, g_post, 0.5, 0, ROWS_MAIN),
                    resnorm_last(f, x, g_post, 0.5, ROWS_MAIN, TAIL_ROWS)), None
        return resnorm(f, x, g_post, g_next, 0.5)

    outs = {k: [] for k in ('kp', 'vp', 'srp', 'sip', 'ks', 'vs', 'srs', 'sis')}
    h = rmsnorm_rows(x[0], x[1], ffn1_norm_pre[0])
    for i in range(DEPTH):
        x, h = ffn(x, h, i, ffn1_w_gate, ffn1_w_up, ffn1_w_down, ffn1_norm_post[i], mix_norm_pre[i])

        proj = matmul_wf32(h, w_in, i)
        sp = _ssm_params(i, ssm_lambda_re, ssm_lambda_im, ssm_log_step, ssm_b_re, ssm_b_im,
                         ssm_c_re, ssm_c_im, ssm_d, ssm_w_glu, ssm_b_glu, ssm_out_norm)
        ys_meta, mr, mi = ssm_mixer(proj, lambda b, ti: tail_blk16, zero_state, zero_state, sp,
                                    n_seq=BATCH, tc=N_META, nt=1, perm=True)
        ys_main, pr, pi = ssm_mixer(proj, lambda b, ti: b * (SEQ // SCAN_TC) + ti, mr, mi, sp,
                                    n_seq=BATCH, tc=SCAN_TC, nt=SEQ // SCAN_TC, perm=True)
        ys_samp, sr, si = ssm_mixer(proj, lambda b, ti: ROWS_MAIN // DEC_BATCH,
                                    state_ssm_re[i].reshape(DEC_BATCH, -1), state_ssm_im[i].reshape(DEC_BATCH, -1),
                                    sp, n_seq=DEC_BATCH, tc=1, nt=1, perm=False)
        pr, pi = pr[:BATCH], pi[:BATCH]
        ys = jnp.concatenate(list(ys_main) + [ys_samp[0], jnp.zeros((n_pad, SSM_WIDTH), BF16), ys_meta[0]], axis=0)

        at_main = prompt_attention(proj, sb_logit_bias[i])
        at_meta = meta_attention(proj, sb_logit_bias[i])
        q_samp = proj[ROWS_MAIN:ROWS_MAIN + DEC_BATCH, SSM_WIDTH:SSM_WIDTH + ATTN_WIDTH]
        at_samp = sample_attention(q_samp.reshape(DEC_BATCH, N_HEADS, HEAD_DIM), cache_k, cache_v, i, page_table,
                                   sb_logit_bias[i]).reshape(DEC_BATCH, ATTN_WIDTH)
        ya = rmsnorm_rows(at_main, tail_of(at_samp, at_meta[META_OFF:]), attn_out_norm[i])

        f = matmul2(ys, ya, w_out, i)
        x, h = resnorm(f, x, mix_norm_post[i], ffn2_norm_pre[i], 1.0)

        g_next = ffn1_norm_pre[i + 1] if i + 1 < DEPTH else None
        x, h = ffn(x, h, i, ffn2_w_gate, ffn2_w_up, ffn2_w_down, ffn2_norm_post[i], g_next)

        k_all, v_all = proj[:, kcol:kcol + ATTN_WIDTH], proj[:, vcol:vcol + ATTN_WIDTH]

        def prompt_rows(t):
            meta = jnp.broadcast_to(t[ROWS_MAIN + META_OFF:][None], (BATCH, N_META, ATTN_WIDTH))
            full = jnp.concatenate([meta, t[:ROWS_MAIN].reshape(BATCH, SEQ, ATTN_WIDTH)], axis=1)
            return full.reshape(BATCH, N_META + SEQ, N_HEADS, HEAD_DIM)

        sample_rows = lambda t: t[ROWS_MAIN:ROWS_MAIN + DEC_BATCH].reshape(DEC_BATCH, 1, N_HEADS, HEAD_DIM)
        outs['kp'].append(prompt_rows(k_all))
        outs['vp'].append(prompt_rows(v_all))
        outs['ks'].append(sample_rows(k_all))
        outs['vs'].append(sample_rows(v_all))
        outs['srp'].append(pr.reshape(BATCH, SSM_GROUPS, SSM_STATE))
        outs['sip'].append(pi.reshape(BATCH, SSM_GROUPS, SSM_STATE))
        outs['srs'].append(sr.reshape(DEC_BATCH, SSM_GROUPS, SSM_STATE))
        outs['sis'].append(si.reshape(DEC_BATCH, SSM_GROUPS, SSM_STATE))

    x_main, x_tail = x
    y_prompt = x_main.reshape(BATCH, SEQ, D_MODEL)
    y_sample = x_tail[:DEC_BATCH].reshape(DEC_BATCH, 1, D_MODEL)
    st = lambda k: jnp.stack(outs[k])
    return (y_prompt, y_sample, st('kp'), st('vp'), st('srp'), st('sip'),
            st('ks'), st('vs'), st('srs'), st('sis'))
```

```python
import functools
import math

import numpy as np
import jax
import jax.numpy as jnp
from jax import lax
from jax.experimental import pallas as pl
from jax.experimental.pallas import tpu as pltpu

F32 = jnp.float32
BF16 = jnp.bfloat16

D_MODEL = 4096
BATCH = 4
SEQ = 2048
DEPTH = 2
DEC_BATCH = 8
PAGE_SIZE = 128
N_HEADS = 16
HEAD_DIM = 128
ATTN_WIDTH = N_HEADS * HEAD_DIM
SSM_CH = 16
SSM_GROUPS = 128
SSM_STATE = 64
SSM_WIDTH = SSM_GROUPS * SSM_CH
D_FF = 11008
N_META = 16
RMS_EPS = 1e-6
LAMBDA_RE_MAX = -1e-4
ATTN_SCALE = 1.0 / math.sqrt(HEAD_DIM)

ROWS_MAIN = BATCH * SEQ
TAIL_ROWS = 128
ROWS = ROWS_MAIN + TAIL_ROWS
META_OFF = TAIL_ROWS - N_META

GB = 16
N_GB = SSM_GROUPS // GB
GB_CH = GB * SSM_CH
GB_ST = GB * SSM_STATE

TM = 640
VMEM_LIMIT = 56 * 1024 * 1024


def _cparams(sem, vmem=VMEM_LIMIT):
    return pltpu.CompilerParams(dimension_semantics=sem, vmem_limit_bytes=vmem)


def _rms(x, gain):
    r = lax.rsqrt(jnp.mean(x * x, axis=-1, keepdims=True) + RMS_EPS)
    return x * r * gain


def _rows_of(main_ref, tail_ref):
    n_main = pl.num_programs(0) - 1
    return jnp.where(pl.program_id(0) < n_main, main_ref[...], tail_ref[...])


def _split_row_specs(main, tm):
    d = main.shape[1]
    last = main.shape[0] // tm - 1
    return [pl.BlockSpec((tm, d), lambda i: (jnp.minimum(i, last), 0)), pl.BlockSpec((tm, d), lambda i: (0, 0))]


def _rmsnorm_kernel(xm_ref, xt_ref, g_ref, o_ref):
    o_ref[...] = _rms(_rows_of(xm_ref, xt_ref), g_ref[...]).astype(o_ref.dtype)


def rmsnorm_rows(x_main, x_tail, gain, out_dtype=BF16):
    tm = TAIL_ROWS
    rows, d = x_main.shape[0] + tm, x_main.shape[1]
    return pl.pallas_call(
        _rmsnorm_kernel,
        out_shape=jax.ShapeDtypeStruct((rows, d), out_dtype),
        grid=(rows // tm,),
        in_specs=_split_row_specs(x_main, tm) + [pl.BlockSpec((1, d), lambda i: (0, 0))],
        out_specs=pl.BlockSpec((tm, d), lambda i: (i, 0)),
        compiler_params=_cparams(("arbitrary",)),
        name="rmsnorm_rows",
    )(x_main, x_tail, gain.reshape(1, d))


def _round_up(x, m):
    return -(-x // m) * m


def _cast_tiling(shape, n_row_blocks, n_col_blocks):
    r, c = shape
    return _round_up(-(-r // n_row_blocks), 16), _round_up(-(-c // n_col_blocks), HEAD_DIM)


def _gateup_kernel(h_ref, wg_ref, wu_ref, *refs, n_cast):
    cast_in, o_ref, cast_out = refs[:n_cast], refs[n_cast], refs[n_cast + 1:]
    h = h_ref[...]
    g = jnp.dot(h, wg_ref[...], preferred_element_type=F32)
    u = jnp.dot(h, wu_ref[...], preferred_element_type=F32)
    o_ref[...] = (g * jax.nn.sigmoid(g) * u).astype(o_ref.dtype)
    for src, dst in zip(cast_in, cast_out):
        dst[...] = src[...].astype(BF16)


TM_BIG = 1664


def gateup(h, wg, wu, casts=(), tm=TM_BIG, tf=256):
    rows, d = h.shape
    f = wg.shape[1]
    grid = (rows // tm, f // tf)
    w_spec = pl.BlockSpec((d, tf), lambda i, j: (0, j))
    cast_in, cast_out, cast_shapes = [], [], []
    for w, layer, row_major in casts:
        r, c = w.shape[1:]
        if row_major:
            br, bc = _cast_tiling((r, c), grid[0], grid[1])
            imap = lambda i, j: (i, j)
        else:
            br, bc = _cast_tiling((r, c), grid[1], grid[0])
            imap = lambda i, j: (j, i)
        cast_in.append(pl.BlockSpec((None, br, bc), functools.partial(lambda i, j, m, l: (l,) + m(i, j), m=imap, l=layer)))
        cast_out.append(pl.BlockSpec((br, bc), imap))
        cast_shapes.append(jax.ShapeDtypeStruct((r, c), BF16))
    out = pl.pallas_call(
        functools.partial(_gateup_kernel, n_cast=len(casts)),
        out_shape=tuple([jax.ShapeDtypeStruct((rows, f), BF16)] + cast_shapes),
        grid=grid,
        in_specs=[pl.BlockSpec((tm, d), lambda i, j: (i, 0)), w_spec, w_spec] + cast_in,
        out_specs=tuple([pl.BlockSpec((tm, tf), lambda i, j: (i, j))] + cast_out),
        compiler_params=_cparams(("arbitrary", "arbitrary")),
        name="ffn_gateup",
    )(h, wg, wu, *[w for w, _, _ in casts])
    return out[0], list(out[1:])


def _cast_kernel(w_ref, o_ref):
    o_ref[...] = w_ref[...].astype(o_ref.dtype)


def cast_weight(w, layer, tr=256):
    r, c = w.shape[1:]
    return pl.pallas_call(
        _cast_kernel,
        out_shape=jax.ShapeDtypeStruct((r, c), BF16),
        grid=(r // tr,),
        in_specs=[pl.BlockSpec((None, tr, c), lambda i: (layer, i, 0))],
        out_specs=pl.BlockSpec((tr, c), lambda i: (i, 0)),
        compiler_params=_cparams(("parallel",)),
        name="cast_weight",
    )(w)


def _mm_kernel(a_ref, w_ref, o_ref):
    o_ref[...] = jnp.dot(a_ref[...], w_ref[...], preferred_element_type=F32).astype(o_ref.dtype)


def matmul(a, w, tm=TM, tn=256, out_dtype=F32):
    rows, k = a.shape
    n = w.shape[1]
    return pl.pallas_call(
        _mm_kernel,
        out_shape=jax.ShapeDtypeStruct((rows, n), out_dtype),
        grid=(rows // tm, n // tn),
        in_specs=[pl.BlockSpec((tm, k), lambda i, j: (i, 0)),
                  pl.BlockSpec((k, tn), lambda i, j: (0, j))],
        out_specs=pl.BlockSpec((tm, tn), lambda i, j: (i, j)),
        compiler_params=_cparams(("parallel", "arbitrary")),
        name="matmul",
    )(a, w)


def _mm2_kernel(a1_ref, a2_ref, w1_ref, w2_ref, o_ref):
    acc = jnp.dot(a1_ref[...], w1_ref[...], preferred_element_type=F32)
    acc = acc + jnp.dot(a2_ref[...], w2_ref[...], preferred_element_type=F32)
    o_ref[...] = acc


def matmul2(a1, a2, w, tm=TM_BIG, tn=512):
    rows, k1 = a1.shape
    k2 = a2.shape[1]
    assert k1 == k2 and w.shape[0] == k1 + k2
    n = w.shape[1]
    return pl.pallas_call(
        _mm2_kernel,
        out_shape=jax.ShapeDtypeStruct((rows, n), F32),
        grid=(rows // tm, n // tn),
        in_specs=[pl.BlockSpec((tm, k1), lambda i, j: (i, 0)),
                  pl.BlockSpec((tm, k2), lambda i, j: (i, 0)),
                  pl.BlockSpec((k1, tn), lambda i, j: (0, j)),
                  pl.BlockSpec((k2, tn), lambda i, j: (1, j))],
        out_specs=pl.BlockSpec((tm, tn), lambda i, j: (i, j)),
        compiler_params=_cparams(("parallel", "arbitrary")),
        name="matmul2",
    )(a1, a2, w, w)


def _resnorm_kernel(f_ref, *refs, scale, split):
    if split:
        xm_ref, xt_ref, gp_ref, gn_ref, xo_ref, ho_ref = refs
        x = _rows_of(xm_ref, xt_ref)
    else:
        x_ref, gp_ref, gn_ref, xo_ref, ho_ref = refs
        x = x_ref[...]
    xn = x + scale * _rms(f_ref[...], gp_ref[...])
    xo_ref[...] = xn
    ho_ref[...] = _rms(xn, gn_ref[...]).astype(ho_ref.dtype)


def resnorm(f, x, gain_post, gain_next, scale):
    tm = TAIL_ROWS
    rows, d = f.shape
    row_spec = pl.BlockSpec((tm, d), lambda i: (i, 0))
    vec_spec = pl.BlockSpec((1, d), lambda i: (0, 0))
    split = isinstance(x, tuple)
    x_specs, x_args = (_split_row_specs(x[0], tm), list(x)) if split else ([row_spec], [x])
    return pl.pallas_call(
        functools.partial(_resnorm_kernel, scale=scale, split=split),
        out_shape=(jax.ShapeDtypeStruct((rows, d), F32), jax.ShapeDtypeStruct((rows, d), BF16)),
        grid=(rows // tm,),
        in_specs=[row_spec] + x_specs + [vec_spec, vec_spec],
        out_specs=(row_spec, row_spec),
        compiler_params=_cparams(("arbitrary",)),
        name="resnorm",
    )(f, *x_args, gain_post.reshape(1, d), gain_next.reshape(1, d))


def _resnorm_last_kernel(f_ref, x_ref, gp_ref, xo_ref, *, scale):
    xo_ref[...] = x_ref[...] + scale * _rms(f_ref[...], gp_ref[...])


def resnorm_last(f, x, gain_post, scale, row0, n_rows, tm=128):
    d = x.shape[1]
    b0 = row0 // tm
    row_spec = pl.BlockSpec((tm, d), lambda i: (b0 + i, 0))
    return pl.pallas_call(
        functools.partial(_resnorm_last_kernel, scale=scale),
        out_shape=jax.ShapeDtypeStruct((n_rows, d), F32),
        grid=(n_rows // tm,),
        in_specs=[row_spec, row_spec, pl.BlockSpec((1, d), lambda i: (0, 0))],
        out_specs=pl.BlockSpec((tm, d), lambda i: (i, 0)),
        compiler_params=_cparams(("parallel",)),
        name="resnorm_last",
    )(f, x, gain_post.reshape(1, d))


def _ssm_discretize_kernel(lre_ref, lim_ref, ls_ref, bre_ref, bim_ref,
                           ar_ref, ai_ref, bbr_ref, bbi_ref):
    step = jnp.exp(ls_ref[...])
    lr = jnp.minimum(lre_ref[...], LAMBDA_RE_MAX)
    li = lim_ref[...]
    mag = jnp.exp(lr * step)
    ar = mag * jnp.cos(li * step)
    ai = mag * jnp.sin(li * step)
    den = lr * lr + li * li
    fr = ((ar - 1.0) * lr + ai * li) / den
    fi = (ai * lr - (ar - 1.0) * li) / den
    br = bre_ref[...]
    bi = bim_ref[...]
    ar_ref[...] = ar
    ai_ref[...] = ai
    bbr_ref[...] = fr * br - fi * bi
    bbi_ref[...] = fr * bi + fi * br


def ssm_discretize(lam_re, lam_im, log_step, b_re, b_im):
    g, p = lam_re.shape
    ch = b_re.shape[1]
    gp = jax.ShapeDtypeStruct((g, 1, p), F32)
    gcp = jax.ShapeDtypeStruct((g, ch, p), F32)
    return pl.pallas_call(
        _ssm_discretize_kernel,
        out_shape=(gp, gp, gcp, gcp),
        name="ssm_discretize",
    )(lam_re.reshape(g, 1, p), lam_im.reshape(g, 1, p), log_step.reshape(g, 1, 1), b_re, b_im)


def _block_diag(x):
    _, r, c = x.shape
    x = x.reshape(N_GB, GB, r, c)
    eye = jnp.eye(GB, dtype=x.dtype)
    return jnp.einsum('bgrc,gh->bgrhc', x, eye).reshape(N_GB, GB * r, GB * c)


SCAN_ROWS = 8
SCAN_TC = 64
GB_PER_STEP = 2


def _perm_matrix(n_seq, tc):
    p = np.zeros((SCAN_ROWS * tc, n_seq * tc), np.float32)
    t, b = np.meshgrid(np.arange(tc), np.arange(n_seq), indexing='ij')
    p[(t * SCAN_ROWS + b).ravel(), (b * tc + t).ravel()] = 1.0
    return p


def _ssm_kernel(*refs, n_seq, tc, perm):
    nb = SCAN_ROWS
    n_u = n_seq if perm else 1
    u_refs = refs[:n_u]
    i = n_u
    if perm:
        p_ref, pt_ref = refs[i], refs[i + 1]
        i += 2
    (s0r_ref, s0i_ref, ar_ref, ai_ref, bb_ref, cc_ref, glu_ref, d_ref, bglu_ref, gain_ref) = refs[i:i + 10]
    i += 10
    y_refs = refs[i:i + n_u]
    i += n_u
    str_ref, sti_ref = refs[i], refs[i + 1]
    uhi_ref, ulo_ref, bu_ref, ss_ref, yacc_ref, carry_ref = refs[i + 2:]

    ti = pl.program_id(0)
    gi = pl.program_id(1)

    @pl.when(gi == 0)
    def _():
        if perm:
            ucat = jnp.concatenate([r[...] for r in u_refs], axis=0)
        else:
            ucat = u_refs[0][...]
        hi = ucat.astype(BF16)
        lo = (ucat - hi.astype(F32)).astype(BF16)
        if perm:
            hi = jnp.dot(p_ref[...], hi, preferred_element_type=F32).astype(BF16)
            lo = jnp.dot(p_ref[...], lo, preferred_element_type=F32).astype(BF16)
        for k in range(N_GB):
            uhi_ref[k] = hi[:, k * GB_CH:(k + 1) * GB_CH]
            ulo_ref[k] = lo[:, k * GB_CH:(k + 1) * GB_CH]

    @pl.when(ti == 0)
    def _():
        for k in range(GB_PER_STEP):
            sl = slice(k * GB_ST, (k + 1) * GB_ST)
            carry_ref[gi * GB_PER_STEP + k] = jnp.concatenate([s0r_ref[:, sl], s0i_ref[:, sl]], axis=-1)

    for k in range(GB_PER_STEP):
        bu_ref[k] = jnp.dot(uhi_ref[gi * GB_PER_STEP + k], bb_ref[k], preferred_element_type=F32)

    for k in range(GB_PER_STEP):
        be = gi * GB_PER_STEP + k
        a_re = jnp.broadcast_to(ar_ref[k], (nb, GB_ST))
        a_im = jnp.broadcast_to(ai_ref[k], (nb, GB_ST))
        s0 = carry_ref[be]
        sr, si = s0[:, :GB_ST], s0[:, GB_ST:]
        for t in range(tc):
            b = bu_ref[k, t * nb:(t + 1) * nb, :]
            sr, si = a_re * sr - a_im * si + b[:, :GB_ST], a_re * si + a_im * sr + b[:, GB_ST:]
            ss_ref[k, t * nb:(t + 1) * nb, :] = jnp.concatenate([sr, si], axis=-1)
        carry_ref[be] = jnp.concatenate([sr, si], axis=-1)
        str_ref[be] = sr
        sti_ref[be] = si

        y = jnp.dot(ss_ref[k].astype(BF16), cc_ref[k], preferred_element_type=F32)
        u32 = uhi_ref[be].astype(F32) + ulo_ref[be].astype(F32)
        y = y + d_ref[k] * u32
        g = jax.nn.gelu(y)
        gate = jnp.dot(g.astype(BF16), glu_ref[k], preferred_element_type=F32) + bglu_ref[k]
        yacc_ref[be] = g * jax.nn.sigmoid(gate)

    @pl.when(gi == N_GB // GB_PER_STEP - 1)
    def _():
        o = jnp.concatenate([yacc_ref[k] for k in range(N_GB)], axis=-1)
        on = _rms(o, gain_ref[...]).astype(BF16)
        if perm:
            on = jnp.dot(pt_ref[...], on, preferred_element_type=F32).astype(BF16)
            for b in range(n_seq):
                y_refs[b][...] = on[b * tc:(b + 1) * tc]
        else:
            y_refs[0][...] = on


def ssm_mixer(u_src, u_row_block, s0_re, s0_im, sp, *, n_seq, tc, nt, perm):
    nb = SCAN_ROWS
    rows = nb * tc
    n_u = n_seq if perm else 1
    blk_rows = tc if perm else rows
    in_specs = [pl.BlockSpec((blk_rows, SSM_WIDTH), functools.partial(lambda ti, be, b: (u_row_block(b, ti), 0), b=b))
                for b in range(n_u)]
    args = [u_src] * n_u
    if perm:
        pm = _perm_matrix(n_seq, tc)
        in_specs += [pl.BlockSpec(pm.shape, lambda ti, be: (0, 0)), pl.BlockSpec(pm.T.shape, lambda ti, be: (0, 0))]
        args += [jnp.asarray(pm, BF16), jnp.asarray(pm.T, BF16)]
    gps = GB_PER_STEP
    st_spec = pl.BlockSpec((nb, gps * GB_ST), lambda ti, gi: (0, gi))
    per_block = lambda r, c: pl.BlockSpec((gps, r, c), lambda ti, gi: (gi, 0, 0))
    in_specs += [
        st_spec, st_spec,
        per_block(1, GB_ST), per_block(1, GB_ST),
        per_block(GB_CH, 2 * GB_ST), per_block(2 * GB_ST, GB_CH), per_block(GB_CH, GB_CH),
        per_block(1, GB_CH), per_block(1, GB_CH),
        pl.BlockSpec((1, SSM_WIDTH), lambda ti, gi: (0, 0)),
    ]
    args += [s0_re, s0_im, sp['ar'], sp['ai'], sp['bb'], sp['cc'], sp['glu'], sp['d'], sp['bglu'], sp['gain']]
    y_shape = jax.ShapeDtypeStruct((nt * blk_rows, SSM_WIDTH), BF16)
    st_shape = jax.ShapeDtypeStruct((N_GB, nb, GB_ST), F32)
    st_out_spec = pl.BlockSpec((N_GB, nb, GB_ST), lambda ti, be: (0, 0, 0))
    out = pl.pallas_call(
        functools.partial(_ssm_kernel, n_seq=n_seq, tc=tc, perm=perm),
        out_shape=tuple([y_shape] * n_u + [st_shape, st_shape]),
        grid=(nt, N_GB // gps),
        in_specs=in_specs,
        out_specs=tuple([pl.BlockSpec((blk_rows, SSM_WIDTH), lambda ti, be: (ti, 0))] * n_u + [st_out_spec, st_out_spec]),
        scratch_shapes=[
            pltpu.VMEM((N_GB, rows, GB_CH), BF16),
            pltpu.VMEM((N_GB, rows, GB_CH), BF16),
            pltpu.VMEM((gps, rows, 2 * GB_ST), F32),
            pltpu.VMEM((gps, rows, 2 * GB_ST), F32),
            pltpu.VMEM((N_GB, rows, GB_CH), F32),
            pltpu.VMEM((N_GB, nb, 2 * GB_ST), F32),
        ],
        compiler_params=_cparams(("arbitrary", "arbitrary")),
        name="ssm_mixer",
    )(*args)
    flat = lambda s: jnp.swapaxes(s, 0, 1).reshape(nb, N_GB * GB_ST)
    return out[:n_u], flat(out[n_u]), flat(out[n_u + 1])


def _softplus(z):
    return jnp.maximum(z, 0.0) + jnp.log(1.0 + jnp.exp(-jnp.abs(z)))


def _split_bf16(x):
    hi = x.astype(BF16)
    return hi, (x - hi.astype(F32)).astype(BF16)


def _sb_block(q, kb, vb, upper, bias, run, acc, mask):
    z = lax.dot_general(q, kb.astype(BF16), (((1,), (1,)), ((), ())), preferred_element_type=F32)
    z = z * ATTN_SCALE + bias
    sp = _softplus(z)
    log_1m = -sp
    if mask is not None:
        log_1m = jnp.where(mask, log_1m, 0.0)
    hi, lo = _split_bf16(log_1m)
    after = (jnp.dot(hi, upper, preferred_element_type=F32)
             + jnp.dot(lo, upper, preferred_element_type=F32))
    a = jnp.exp((z - sp) + after + run)
    if mask is not None:
        a = jnp.where(mask, a, 0.0)
    acc = acc + jnp.dot(a.astype(BF16), vb.astype(BF16), preferred_element_type=F32)
    run = run + jnp.sum(log_1m, axis=1, keepdims=True)
    return run, acc


def _upper_matrix(n):
    j, s = np.meshgrid(np.arange(n), np.arange(n), indexing='ij')
    return (j > s).astype(np.float32)


def _upper_ext_matrix(n):
    return np.concatenate([_upper_matrix(n), np.ones((n, HEAD_DIM), np.float32)], axis=1)


LOG2E = 1.4426950408889634


def _sb_block_ext(q2, kb, vb, neg_upper_ext, bias2, run, mask):
    tk = kb.shape[0]
    w = lax.dot_general(q2, kb.astype(BF16), (((1,), (1,)), ((), ())), preferred_element_type=F32) + bias2
    sp2 = jnp.maximum(w, 0.0) + jnp.log2(1.0 + jnp.exp2(-jnp.abs(w)))
    if mask is not None:
        sp2 = jnp.where(mask, sp2, 0.0)
    ext = jnp.dot(sp2.astype(BF16), neg_upper_ext, preferred_element_type=F32)
    run_wide = jnp.concatenate([run] * (tk // HEAD_DIM), axis=1)
    a = jnp.exp2((w - sp2) + ext[:, :tk] + run_wide)
    if mask is not None:
        a = jnp.where(mask, a, 0.0)
    pv = jnp.dot(a.astype(BF16), vb.astype(BF16), preferred_element_type=F32)
    return run + ext[:, tk:], pv


HEADS_PER_STEP = 8


def _prompt_attn_kernel(bias_ref, q_ref, k_ref, v_ref, km_ref, vm_ref, up_ref, upm_ref, o_ref, run_ref, *, tb):
    hb = HEADS_PER_STEP
    hg = pl.program_id(1)
    iq = pl.program_id(2)
    heads = [slice(hh * HEAD_DIM, (hh + 1) * HEAD_DIM) for hh in range(hb)]
    bias = [bias_ref[hg * hb + hh] * LOG2E for hh in range(hb)]
    q = [(q_ref[:, sl] * (ATTN_SCALE * LOG2E)).astype(BF16) for sl in heads]
    row = lax.broadcasted_iota(jnp.int32, (tb, tb), 0)
    col = lax.broadcasted_iota(jnp.int32, (tb, tb), 1)

    d0 = pl.multiple_of(iq * tb, tb)
    for hh, sl in enumerate(heads):
        run, pv = _sb_block_ext(q[hh], k_ref[pl.ds(d0, tb), sl], v_ref[pl.ds(d0, tb), sl], up_ref[...], bias[hh],
                                jnp.zeros((tb, HEAD_DIM), F32), col < row)
        run_ref[hh] = run
        o_ref[:, sl] = pv

    def body(n, c):
        k0 = pl.multiple_of((iq - 1 - n) * tb, tb)
        for hh, sl in enumerate(heads):
            run, pv = _sb_block_ext(q[hh], k_ref[pl.ds(k0, tb), sl], v_ref[pl.ds(k0, tb), sl], up_ref[...], bias[hh],
                                    run_ref[hh], None)
            run_ref[hh] = run
            o_ref[:, sl] += pv
        return c

    lax.fori_loop(0, iq, body, 0)

    colm = lax.broadcasted_iota(jnp.int32, (tb, TAIL_ROWS), 1)
    for hh, sl in enumerate(heads):
        _, pv = _sb_block_ext(q[hh], km_ref[:, sl], vm_ref[:, sl], upm_ref[...], bias[hh], run_ref[hh],
                              colm >= META_OFF)
        o_ref[:, sl] += pv


def prompt_attention(proj, bias, seq=SEQ, n_batch=BATCH, tb=256):
    hb = HEADS_PER_STEP
    wb = hb * HEAD_DIM
    nqb = seq // tb
    qcol, kcol, vcol = SSM_WIDTH // wb, (SSM_WIDTH + ATTN_WIDTH) // wb, (SSM_WIDTH + 2 * ATTN_WIDTH) // wb
    tail_blk = (n_batch * seq) // TAIL_ROWS
    const = lambda a: pl.BlockSpec(a.shape, lambda b, h, i: (0, 0))
    up = jnp.asarray(-_upper_ext_matrix(tb), BF16)
    upm = jnp.asarray(-_upper_ext_matrix(TAIL_ROWS), BF16)
    return pl.pallas_call(
        functools.partial(_prompt_attn_kernel, tb=tb),
        out_shape=jax.ShapeDtypeStruct((n_batch * seq, ATTN_WIDTH), F32),
        grid=(n_batch, N_HEADS // hb, nqb),
        in_specs=[
            pl.BlockSpec(memory_space=pltpu.SMEM),
            pl.BlockSpec((tb, wb), lambda b, h, i: (b * nqb + i, qcol + h)),
            pl.BlockSpec((seq, wb), lambda b, h, i: (b, kcol + h)),
            pl.BlockSpec((seq, wb), lambda b, h, i: (b, vcol + h)),
            pl.BlockSpec((TAIL_ROWS, wb), lambda b, h, i: (tail_blk, kcol + h)),
            pl.BlockSpec((TAIL_ROWS, wb), lambda b, h, i: (tail_blk, vcol + h)),
            const(up), const(upm),
        ],
        out_specs=pl.BlockSpec((tb, wb), lambda b, h, i: (b * nqb + i, h)),
        scratch_shapes=[pltpu.VMEM((hb, tb, HEAD_DIM), F32)],
        compiler_params=_cparams(("parallel", "parallel", "arbitrary")),
        name="prompt_attention",
    )(bias, proj, proj, proj, proj, proj, up, upm)


def _meta_attn_kernel(bias_ref, q_ref, k_ref, v_ref, up_ref, o_ref):
    bias = bias_ref[pl.program_id(0)]
    row = lax.broadcasted_iota(jnp.int32, (TAIL_ROWS, TAIL_ROWS), 0)
    col = lax.broadcasted_iota(jnp.int32, (TAIL_ROWS, TAIL_ROWS), 1)
    mask = (col < row) & (col >= META_OFF)
    run = jnp.zeros((TAIL_ROWS, 1), F32)
    acc = jnp.zeros((TAIL_ROWS, HEAD_DIM), F32)
    _, acc = _sb_block(q_ref[...].astype(BF16), k_ref[...], v_ref[...], up_ref[...], bias, run, acc, mask)
    o_ref[...] = acc


def meta_attention(proj, bias, tail_blk=ROWS_MAIN // TAIL_ROWS):
    qcol, kcol, vcol = (SSM_WIDTH // HEAD_DIM, (SSM_WIDTH + ATTN_WIDTH) // HEAD_DIM,
                        (SSM_WIDTH + 2 * ATTN_WIDTH) // HEAD_DIM)
    blk = lambda c0: pl.BlockSpec((TAIL_ROWS, HEAD_DIM), lambda h: (tail_blk, c0 + h))
    return pl.pallas_call(
        _meta_attn_kernel,
        out_shape=jax.ShapeDtypeStruct((TAIL_ROWS, ATTN_WIDTH), F32),
        grid=(N_HEADS,),
        in_specs=[pl.BlockSpec(memory_space=pltpu.SMEM), blk(qcol), blk(kcol), blk(vcol),
                  pl.BlockSpec((TAIL_ROWS, TAIL_ROWS), lambda h: (0, 0))],
        out_specs=pl.BlockSpec((TAIL_ROWS, HEAD_DIM), lambda h: (0, h)),
        compiler_params=_cparams(("parallel",)),
        name="meta_attention",
    )(bias, proj, proj, proj, jnp.asarray(_upper_matrix(TAIL_ROWS), BF16))


PAGES_PER_STEP = 8


def _sample_attn_kernel(pt_ref, q_ref, *refs, n_steps):
    npg = PAGES_PER_STEP
    k_refs = refs[:npg]
    v_refs = refs[npg:2 * npg]
    g_ref, gt_ref, up_ref, bias_ref, o_ref, run_ref = refs[2 * npg:]
    jp = pl.program_id(1)
    nh = N_HEADS
    flat = PAGE_SIZE * nh

    @pl.when(jp == 0)
    def _():
        o_ref[...] = jnp.zeros_like(o_ref)
        run_ref[...] = jnp.zeros_like(run_ref)

    q = q_ref[...].astype(BF16)
    row = lax.broadcasted_iota(jnp.int32, (nh, flat), 0)
    col = lax.broadcasted_iota(jnp.int32, (nh, flat), 1)
    own = (col & (nh - 1)) == row

    zs = []
    for p in range(npg):
        kp = k_refs[p][...].reshape(flat, HEAD_DIM).astype(BF16)
        zt = lax.dot_general(q, kp, (((1,), (1,)), ((), ())), preferred_element_type=F32)
        zs.append(jnp.where(own, zt, 0.0))
    hi, lo = _split_bf16(jnp.concatenate(zs, axis=0))
    zc = jnp.dot(jnp.concatenate([hi, lo], axis=0), g_ref[...], preferred_element_type=F32)
    z = (zc[:npg * nh] + zc[npg * nh:]) * ATTN_SCALE + bias_ref[...]
    sp = _softplus(z)
    log_1m = -sp
    hi, lo = _split_bf16(log_1m)
    ext = jnp.dot(jnp.concatenate([hi, lo], axis=0), up_ref[...], preferred_element_type=F32)
    ext = ext[:npg * nh] + ext[npg * nh:]
    after, total = ext[:, :PAGE_SIZE], ext[:, PAGE_SIZE:]
    run = run_ref[...]
    runs = []
    for p in range(npg):
        runs.append(run)
        run = run + total[p * nh:(p + 1) * nh]
    run_ref[...] = run
    a = jnp.exp((z - sp) + after + jnp.concatenate(runs, axis=0))
    aw = jnp.dot(a.astype(BF16), gt_ref[...], preferred_element_type=F32)
    acc = o_ref[...]
    for p in range(npg):
        am = jnp.where(own, aw[p * nh:(p + 1) * nh], 0.0).astype(BF16)
        vp = v_refs[p][...].reshape(flat, HEAD_DIM).astype(BF16)
        acc = acc + jnp.dot(am, vp, preferred_element_type=F32)
    o_ref[...] = acc


def sample_attention(q, cache_k, cache_v, layer, page_table, bias):
    n_b, n_pages = page_table.shape
    npg = PAGES_PER_STEP
    n_steps = n_pages // npg
    flat = PAGE_SIZE * N_HEADS
    g = (np.arange(flat)[:, None] // N_HEADS == np.arange(PAGE_SIZE)[None, :]).astype(np.float32)
    bias_rows = jnp.tile(jnp.broadcast_to(bias[:, None], (N_HEADS, PAGE_SIZE)), (npg, 1))

    def page_spec(p):
        return pl.BlockSpec((None, None, PAGE_SIZE, N_HEADS, HEAD_DIM),
                            lambda b, jp, pt: (layer, pt[b, n_pages - 1 - (jp * npg + p)], 0, 0, 0))

    const = lambda shape: pl.BlockSpec(shape, lambda b, jp, pt: (0, 0))
    head_spec = pl.BlockSpec((None, N_HEADS, HEAD_DIM), lambda b, jp, pt: (b, 0, 0))
    grid_spec = pltpu.PrefetchScalarGridSpec(
        num_scalar_prefetch=1,
        grid=(n_b, n_steps),
        in_specs=([head_spec] + [page_spec(p) for p in range(npg)] * 2
                  + [const((flat, PAGE_SIZE)), const((PAGE_SIZE, flat)),
                     const((PAGE_SIZE, PAGE_SIZE + HEAD_DIM)), const((npg * N_HEADS, PAGE_SIZE))]),
        out_specs=head_spec,
        scratch_shapes=[pltpu.VMEM((N_HEADS, HEAD_DIM), F32)],
    )
    return pl.pallas_call(
        functools.partial(_sample_attn_kernel, n_steps=n_steps),
        out_shape=jax.ShapeDtypeStruct((n_b, N_HEADS, HEAD_DIM), F32),
        grid_spec=grid_spec,
        compiler_params=_cparams(("arbitrary", "arbitrary")),
        name="sample_attention",
    )(page_table, q, *([cache_k] * npg), *([cache_v] * npg),
      jnp.asarray(g, BF16), jnp.asarray(g.T, BF16), jnp.asarray(_upper_ext_matrix(PAGE_SIZE), BF16), bias_rows)


def _ssm_params(i, lam_re, lam_im, log_step, b_re, b_im, c_re, c_im, d, w_glu, b_glu, out_gain):
    ar, ai, bbr, bbi = ssm_discretize(lam_re[i], lam_im[i], log_step[i],
                                      jnp.swapaxes(b_re[i], 1, 2), jnp.swapaxes(b_im[i], 1, 2))
    bb = jnp.concatenate([_block_diag(bbr), _block_diag(bbi)], axis=-1).astype(BF16)
    cc = jnp.concatenate([_block_diag(jnp.swapaxes(c_re[i], 1, 2)),
                          -_block_diag(jnp.swapaxes(c_im[i], 1, 2))], axis=1).astype(BF16)
    return dict(
        ar=ar.reshape(N_GB, 1, GB_ST), ai=ai.reshape(N_GB, 1, GB_ST), bb=bb, cc=cc,
        glu=_block_diag(w_glu[i]).astype(BF16),
        d=d[i].reshape(N_GB, 1, GB_CH), bglu=b_glu[i].reshape(N_GB, 1, GB_CH),
        gain=out_gain[i].reshape(1, SSM_WIDTH))


def kernel(x_prompt, x_sample, cache_k, cache_v, state_ssm_re, state_ssm_im, page_table, meta_tokens,
           ffn1_norm_pre, ffn1_norm_post, ffn1_w_gate, ffn1_w_up, ffn1_w_down,
           mix_norm_pre, mix_norm_post, w_in, sb_logit_bias,
           ssm_lambda_re, ssm_lambda_im, ssm_log_step, ssm_b_re, ssm_b_im, ssm_c_re, ssm_c_im,
           ssm_d, ssm_w_glu, ssm_b_glu, ssm_out_norm, attn_out_norm, w_out,
           ffn2_norm_pre, ffn2_norm_post, ffn2_w_gate, ffn2_w_up, ffn2_w_down):
    n_pad = TAIL_ROWS - DEC_BATCH - N_META
    tail_of = lambda sample, meta: jnp.concatenate([sample, jnp.zeros((n_pad, sample.shape[1]), sample.dtype), meta], 0)
    x = (x_prompt.reshape(ROWS_MAIN, D_MODEL), tail_of(x_sample.reshape(DEC_BATCH, D_MODEL), meta_tokens.astype(F32)))
    zero_state = jnp.zeros((SCAN_ROWS, SSM_GROUPS * SSM_STATE), F32)
    tail_blk16 = (ROWS_MAIN + META_OFF) // N_META
    kcol, vcol = SSM_WIDTH + ATTN_WIDTH, SSM_WIDTH + 2 * ATTN_WIDTH

    def ffn(x, h, w_bf16, casts, g_post, g_next):
        a, cast = gateup(h, w_bf16[0], w_bf16[1], casts)
        f = matmul(a, w_bf16[2])
        if g_next is None:
            return (resnorm_last(f, x, g_post, 0.5, 0, ROWS_MAIN),
                    resnorm_last(f, x, g_post, 0.5, ROWS_MAIN, TAIL_ROWS)), None, cast
        return resnorm(f, x, g_post, g_next, 0.5) + (cast,)

    ffn_casts = lambda wg, wu, wd, i: [(wg, i, True), (wu, i, True), (wd, i, False)]

    outs = {k: [] for k in ('kp', 'vp', 'srp', 'sip', 'ks', 'vs', 'srs', 'sis')}
    h = rmsnorm_rows(x[0], x[1], ffn1_norm_pre[0])
    w_ffn1 = [cast_weight(w, 0) for w in (ffn1_w_gate, ffn1_w_up, ffn1_w_down)]
    for i in range(DEPTH):
        casts = [(w_in, i, False), (w_out, i, False)] + ffn_casts(ffn2_w_gate, ffn2_w_up, ffn2_w_down, i)
        x, h, (w_in_b, w_out_b, *w_ffn2) = ffn(x, h, w_ffn1, casts, ffn1_norm_post[i], mix_norm_pre[i])

        proj = matmul(h, w_in_b, tm=TM_BIG, tn=512)
        sp = _ssm_params(i, ssm_lambda_re, ssm_lambda_im, ssm_log_step, ssm_b_re, ssm_b_im,
                         ssm_c_re, ssm_c_im, ssm_d, ssm_w_glu, ssm_b_glu, ssm_out_norm)
        ys_meta, mr, mi = ssm_mixer(proj, lambda b, ti: tail_blk16, zero_state, zero_state, sp,
                                    n_seq=BATCH, tc=N_META, nt=1, perm=True)
        ys_main, pr, pi = ssm_mixer(proj, lambda b, ti: b * (SEQ // SCAN_TC) + ti, mr, mi, sp,
                                    n_seq=BATCH, tc=SCAN_TC, nt=SEQ // SCAN_TC, perm=True)
        ys_samp, sr, si = ssm_mixer(proj, lambda b, ti: ROWS_MAIN // DEC_BATCH,
                                    state_ssm_re[i].reshape(DEC_BATCH, -1), state_ssm_im[i].reshape(DEC_BATCH, -1),
                                    sp, n_seq=DEC_BATCH, tc=1, nt=1, perm=False)
        pr, pi = pr[:BATCH], pi[:BATCH]
        ys = jnp.concatenate(list(ys_main) + [ys_samp[0], jnp.zeros((n_pad, SSM_WIDTH), BF16), ys_meta[0]], axis=0)

        at_main = prompt_attention(proj, sb_logit_bias[i])
        at_meta = meta_attention(proj, sb_logit_bias[i])
        q_samp = proj[ROWS_MAIN:ROWS_MAIN + DEC_BATCH, SSM_WIDTH:SSM_WIDTH + ATTN_WIDTH]
        at_samp = sample_attention(q_samp.reshape(DEC_BATCH, N_HEADS, HEAD_DIM), cache_k, cache_v, i, page_table,
                                   sb_logit_bias[i]).reshape(DEC_BATCH, ATTN_WIDTH)
        ya = rmsnorm_rows(at_main, tail_of(at_samp, at_meta[META_OFF:]), attn_out_norm[i])

        f = matmul2(ys, ya, w_out_b)
        x, h = resnorm(f, x, mix_norm_post[i], ffn2_norm_pre[i], 1.0)

        last = i + 1 == DEPTH
        casts = [] if last else ffn_casts(ffn1_w_gate, ffn1_w_up, ffn1_w_down, i + 1)
        x, h, w_ffn1 = ffn(x, h, w_ffn2, casts, ffn2_norm_post[i], None if last else ffn1_norm_pre[i + 1])

        k_all, v_all = proj[:, kcol:kcol + ATTN_WIDTH], proj[:, vcol:vcol + ATTN_WIDTH]

        def prompt_rows(t):
            meta = jnp.broadcast_to(t[ROWS_MAIN + META_OFF:][None], (BATCH, N_META, ATTN_WIDTH))
            full = jnp.concatenate([meta, t[:ROWS_MAIN].reshape(BATCH, SEQ, ATTN_WIDTH)], axis=1)
            return full.reshape(BATCH, N_META + SEQ, N_HEADS, HEAD_DIM)

        sample_rows = lambda t: t[ROWS_MAIN:ROWS_MAIN + DEC_BATCH].reshape(DEC_BATCH, 1, N_HEADS, HEAD_DIM)
        outs['kp'].append(prompt_rows(k_all))
        outs['vp'].append(prompt_rows(v_all))
        outs['ks'].append(sample_rows(k_all))
        outs['vs'].append(sample_rows(v_all))
        outs['srp'].append(pr.reshape(BATCH, SSM_GROUPS, SSM_STATE))
        outs['sip'].append(pi.reshape(BATCH, SSM_GROUPS, SSM_STATE))
        outs['srs'].append(sr.reshape(DEC_BATCH, SSM_GROUPS, SSM_STATE))
        outs['sis'].append(si.reshape(DEC_BATCH, SSM_GROUPS, SSM_STATE))

    x_main, x_tail = x
    y_prompt = x_main.reshape(BATCH, SEQ, D_MODEL)
    y_sample = x_tail[:DEC_BATCH].reshape(DEC_BATCH, 1, D_MODEL)
    st = lambda k: jnp.stack(outs[k])
    return (y_prompt, y_sample, st('kp'), st('vp'), st('srp'), st('sip'),
            st('ks'), st('vs'), st('srs'), st('sis'))
```

```python
import functools
import math

import numpy as np
import jax
import jax.numpy as jnp
from jax import lax
from jax.experimental import pallas as pl
from jax.experimental.pallas import tpu as pltpu

F32 = jnp.float32
BF16 = jnp.bfloat16

D_MODEL = 4096
BATCH = 4
SEQ = 2048
DEPTH = 2
DEC_BATCH = 8
PAGE_SIZE = 128
N_HEADS = 16
HEAD_DIM = 128
ATTN_WIDTH = N_HEADS * HEAD_DIM
SSM_CH = 16
SSM_GROUPS = 128
SSM_STATE = 64
SSM_WIDTH = SSM_GROUPS * SSM_CH
D_FF = 11008
N_META = 16
RMS_EPS = 1e-6
LAMBDA_RE_MAX = -1e-4
ATTN_SCALE = 1.0 / math.sqrt(HEAD_DIM)

ROWS_MAIN = BATCH * SEQ
TAIL_ROWS = 128
ROWS = ROWS_MAIN + TAIL_ROWS
META_OFF = TAIL_ROWS - N_META

GB = 16
N_GB = SSM_GROUPS // GB
GB_CH = GB * SSM_CH
GB_ST = GB * SSM_STATE

TM = 640
VMEM_LIMIT = 56 * 1024 * 1024


def _cparams(sem, vmem=VMEM_LIMIT):
    return pltpu.CompilerParams(dimension_semantics=sem, vmem_limit_bytes=vmem)


def _rms(x, gain):
    r = lax.rsqrt(jnp.mean(x * x, axis=-1, keepdims=True) + RMS_EPS)
    return x * r * gain


def _rows_of(main_ref, tail_ref):
    n_main = pl.num_programs(0) - 1
    return jnp.where(pl.program_id(0) < n_main, main_ref[...], tail_ref[...])


def _split_row_specs(main, tm):
    d = main.shape[1]
    last = main.shape[0] // tm - 1
    return [pl.BlockSpec((tm, d), lambda i: (jnp.minimum(i, last), 0)), pl.BlockSpec((tm, d), lambda i: (0, 0))]


def _rmsnorm_kernel(xm_ref, xt_ref, g_ref, o_ref):
    o_ref[...] = _rms(_rows_of(xm_ref, xt_ref), g_ref[...]).astype(o_ref.dtype)


def rmsnorm_rows(x_main, x_tail, gain, out_dtype=BF16):
    tm = TAIL_ROWS
    rows, d = x_main.shape[0] + tm, x_main.shape[1]
    return pl.pallas_call(
        _rmsnorm_kernel,
        out_shape=jax.ShapeDtypeStruct((rows, d), out_dtype),
        grid=(rows // tm,),
        in_specs=_split_row_specs(x_main, tm) + [pl.BlockSpec((1, d), lambda i: (0, 0))],
        out_specs=pl.BlockSpec((tm, d), lambda i: (i, 0)),
        compiler_params=_cparams(("arbitrary",)),
        name="rmsnorm_rows",
    )(x_main, x_tail, gain.reshape(1, d))


def _round_up(x, m):
    return -(-x // m) * m


def _cast_tiling(shape, n_row_blocks, n_col_blocks):
    r, c = shape
    return _round_up(-(-r // n_row_blocks), 16), _round_up(-(-c // n_col_blocks), HEAD_DIM)


def _gateup_kernel(h_ref, wg_ref, wu_ref, *refs, n_cast):
    cast_in, o_ref, cast_out = refs[:n_cast], refs[n_cast], refs[n_cast + 1:]
    h = h_ref[...]
    g = jnp.dot(h, wg_ref[...], preferred_element_type=F32)
    u = jnp.dot(h, wu_ref[...], preferred_element_type=F32)
    o_ref[...] = (g * jax.nn.sigmoid(g) * u).astype(o_ref.dtype)
    for src, dst in zip(cast_in, cast_out):
        dst[...] = src[...].astype(BF16)


TM_BIG = 1664


def gateup(h, wg, wu, casts=(), tm=TM_BIG, tf=256):
    rows, d = h.shape
    f = wg.shape[1]
    grid = (rows // tm, f // tf)
    w_spec = pl.BlockSpec((d, tf), lambda i, j: (0, j))
    cast_in, cast_out, cast_shapes = [], [], []
    for w, layer, row_major in casts:
        r, c = w.shape[1:]
        if row_major:
            br, bc = _cast_tiling((r, c), grid[0], grid[1])
            imap = lambda i, j: (i, j)
        else:
            br, bc = _cast_tiling((r, c), grid[1], grid[0])
            imap = lambda i, j: (j, i)
        cast_in.append(pl.BlockSpec((None, br, bc), functools.partial(lambda i, j, m, l: (l,) + m(i, j), m=imap, l=layer)))
        cast_out.append(pl.BlockSpec((br, bc), imap))
        cast_shapes.append(jax.ShapeDtypeStruct((r, c), BF16))
    out = pl.pallas_call(
        functools.partial(_gateup_kernel, n_cast=len(casts)),
        out_shape=tuple([jax.ShapeDtypeStruct((rows, f), BF16)] + cast_shapes),
        grid=grid,
        in_specs=[pl.BlockSpec((tm, d), lambda i, j: (i, 0)), w_spec, w_spec] + cast_in,
        out_specs=tuple([pl.BlockSpec((tm, tf), lambda i, j: (i, j))] + cast_out),
        compiler_params=_cparams(("arbitrary", "arbitrary")),
        name="ffn_gateup",
    )(h, wg, wu, *[w for w, _, _ in casts])
    return out[0], list(out[1:])


def _cast_kernel(w_ref, o_ref):
    o_ref[...] = w_ref[...].astype(o_ref.dtype)


def cast_weight(w, layer, tr=256):
    r, c = w.shape[1:]
    return pl.pallas_call(
        _cast_kernel,
        out_shape=jax.ShapeDtypeStruct((r, c), BF16),
        grid=(r // tr,),
        in_specs=[pl.BlockSpec((None, tr, c), lambda i: (layer, i, 0))],
        out_specs=pl.BlockSpec((tr, c), lambda i: (i, 0)),
        compiler_params=_cparams(("parallel",)),
        name="cast_weight",
    )(w)


def _mm_kernel(a_ref, w_ref, o_ref):
    o_ref[...] = jnp.dot(a_ref[...], w_ref[...], preferred_element_type=F32).astype(o_ref.dtype)


def matmul(a, w, tm=TM, tn=256, out_dtype=F32):
    rows, k = a.shape
    n = w.shape[1]
    return pl.pallas_call(
        _mm_kernel,
        out_shape=jax.ShapeDtypeStruct((rows, n), out_dtype),
        grid=(rows // tm, n // tn),
        in_specs=[pl.BlockSpec((tm, k), lambda i, j: (i, 0)),
                  pl.BlockSpec((k, tn), lambda i, j: (0, j))],
        out_specs=pl.BlockSpec((tm, tn), lambda i, j: (i, j)),
        compiler_params=_cparams(("parallel", "arbitrary")),
        name="matmul",
    )(a, w)


def _mm2_kernel(a1_ref, a2_ref, w1_ref, w2_ref, o_ref):
    acc = jnp.dot(a1_ref[...], w1_ref[...], preferred_element_type=F32)
    acc = acc + jnp.dot(a2_ref[...], w2_ref[...], preferred_element_type=F32)
    o_ref[...] = acc


def matmul2(a1, a2, w, tm=TM_BIG, tn=512):
    rows, k1 = a1.shape
    k2 = a2.shape[1]
    assert k1 == k2 and w.shape[0] == k1 + k2
    n = w.shape[1]
    return pl.pallas_call(
        _mm2_kernel,
        out_shape=jax.ShapeDtypeStruct((rows, n), F32),
        grid=(rows // tm, n // tn),
        in_specs=[pl.BlockSpec((tm, k1), lambda i, j: (i, 0)),
                  pl.BlockSpec((tm, k2), lambda i, j: (i, 0)),
                  pl.BlockSpec((k1, tn), lambda i, j: (0, j)),
                  pl.BlockSpec((k2, tn), lambda i, j: (1, j))],
        out_specs=pl.BlockSpec((tm, tn), lambda i, j: (i, j)),
        compiler_params=_cparams(("parallel", "arbitrary")),
        name="matmul2",
    )(a1, a2, w, w)


def _resnorm_kernel(f_ref, *refs, scale, split):
    if split:
        xm_ref, xt_ref, gp_ref, gn_ref, xo_ref, ho_ref = refs
        x = _rows_of(xm_ref, xt_ref)
    else:
        x_ref, gp_ref, gn_ref, xo_ref, ho_ref = refs
        x = x_ref[...]
    xn = x + scale * _rms(f_ref[...], gp_ref[...])
    xo_ref[...] = xn
    ho_ref[...] = _rms(xn, gn_ref[...]).astype(ho_ref.dtype)


def resnorm(f, x, gain_post, gain_next, scale):
    tm = TAIL_ROWS
    rows, d = f.shape
    row_spec = pl.BlockSpec((tm, d), lambda i: (i, 0))
    vec_spec = pl.BlockSpec((1, d), lambda i: (0, 0))
    split = isinstance(x, tuple)
    x_specs, x_args = (_split_row_specs(x[0], tm), list(x)) if split else ([row_spec], [x])
    return pl.pallas_call(
        functools.partial(_resnorm_kernel, scale=scale, split=split),
        out_shape=(jax.ShapeDtypeStruct((rows, d), F32), jax.ShapeDtypeStruct((rows, d), BF16)),
        grid=(rows // tm,),
        in_specs=[row_spec] + x_specs + [vec_spec, vec_spec],
        out_specs=(row_spec, row_spec),
        compiler_params=_cparams(("arbitrary",)),
        name="resnorm",
    )(f, *x_args, gain_post.reshape(1, d), gain_next.reshape(1, d))


def _resnorm_last_kernel(f_ref, x_ref, gp_ref, xo_ref, *, scale):
    xo_ref[...] = x_ref[...] + scale * _rms(f_ref[...], gp_ref[...])


def resnorm_last(f, x, gain_post, scale, row0, n_rows, tm=128):
    d = x.shape[1]
    b0 = row0 // tm
    row_spec = pl.BlockSpec((tm, d), lambda i: (b0 + i, 0))
    return pl.pallas_call(
        functools.partial(_resnorm_last_kernel, scale=scale),
        out_shape=jax.ShapeDtypeStruct((n_rows, d), F32),
        grid=(n_rows // tm,),
        in_specs=[row_spec, row_spec, pl.BlockSpec((1, d), lambda i: (0, 0))],
        out_specs=pl.BlockSpec((tm, d), lambda i: (i, 0)),
        compiler_params=_cparams(("parallel",)),
        name="resnorm_last",
    )(f, x, gain_post.reshape(1, d))


def _ssm_discretize_kernel(lre_ref, lim_ref, ls_ref, bre_ref, bim_ref,
                           ar_ref, ai_ref, bbr_ref, bbi_ref):
    step = jnp.exp(ls_ref[...])
    lr = jnp.minimum(lre_ref[...], LAMBDA_RE_MAX)
    li = lim_ref[...]
    mag = jnp.exp(lr * step)
    ar = mag * jnp.cos(li * step)
    ai = mag * jnp.sin(li * step)
    den = lr * lr + li * li
    fr = ((ar - 1.0) * lr + ai * li) / den
    fi = (ai * lr - (ar - 1.0) * li) / den
    br = bre_ref[...]
    bi = bim_ref[...]
    ar_ref[...] = ar
    ai_ref[...] = ai
    bbr_ref[...] = fr * br - fi * bi
    bbi_ref[...] = fr * bi + fi * br


def ssm_discretize(lam_re, lam_im, log_step, b_re, b_im):
    g, p = lam_re.shape
    ch = b_re.shape[1]
    gp = jax.ShapeDtypeStruct((g, 1, p), F32)
    gcp = jax.ShapeDtypeStruct((g, ch, p), F32)
    return pl.pallas_call(
        _ssm_discretize_kernel,
        out_shape=(gp, gp, gcp, gcp),
        name="ssm_discretize",
    )(lam_re.reshape(g, 1, p), lam_im.reshape(g, 1, p), log_step.reshape(g, 1, 1), b_re, b_im)


def _block_diag(x):
    _, r, c = x.shape
    x = x.reshape(N_GB, GB, r, c)
    eye = jnp.eye(GB, dtype=x.dtype)
    return jnp.einsum('bgrc,gh->bgrhc', x, eye).reshape(N_GB, GB * r, GB * c)


SCAN_ROWS = 8
SCAN_TC = 64
GB_PER_STEP = 2


def _perm_matrix(n_seq, tc):
    p = np.zeros((SCAN_ROWS * tc, n_seq * tc), np.float32)
    t, b = np.meshgrid(np.arange(tc), np.arange(n_seq), indexing='ij')
    p[(t * SCAN_ROWS + b).ravel(), (b * tc + t).ravel()] = 1.0
    return p


def _ssm_kernel(*refs, n_seq, tc, perm):
    nb = SCAN_ROWS
    n_u = n_seq if perm else 1
    u_refs = refs[:n_u]
    i = n_u
    if perm:
        p_ref, pt_ref = refs[i], refs[i + 1]
        i += 2
    (s0r_ref, s0i_ref, ar_ref, ai_ref, bb_ref, cc_ref, glu_ref, d_ref, bglu_ref, gain_ref) = refs[i:i + 10]
    i += 10
    y_refs = refs[i:i + n_u]
    i += n_u
    str_ref, sti_ref = refs[i], refs[i + 1]
    uhi_ref, ulo_ref, bu_ref, ss_ref, yacc_ref, carry_ref = refs[i + 2:]

    ti = pl.program_id(0)
    gi = pl.program_id(1)

    @pl.when(gi == 0)
    def _():
        if perm:
            ucat = jnp.concatenate([r[...] for r in u_refs], axis=0)
        else:
            ucat = u_refs[0][...]
        hi = ucat.astype(BF16)
        lo = (ucat - hi.astype(F32)).astype(BF16)
        if perm:
            hi = jnp.dot(p_ref[...], hi, preferred_element_type=F32).astype(BF16)
            lo = jnp.dot(p_ref[...], lo, preferred_element_type=F32).astype(BF16)
        for k in range(N_GB):
            uhi_ref[k] = hi[:, k * GB_CH:(k + 1) * GB_CH]
            ulo_ref[k] = lo[:, k * GB_CH:(k + 1) * GB_CH]

    @pl.when(ti == 0)
    def _():
        for k in range(GB_PER_STEP):
            sl = slice(k * GB_ST, (k + 1) * GB_ST)
            carry_ref[gi * GB_PER_STEP + k] = jnp.concatenate([s0r_ref[:, sl], s0i_ref[:, sl]], axis=-1)

    for k in range(GB_PER_STEP):
        bu_ref[k] = jnp.dot(uhi_ref[gi * GB_PER_STEP + k], bb_ref[k], preferred_element_type=F32)

    for k in range(GB_PER_STEP):
        be = gi * GB_PER_STEP + k
        a_re = jnp.broadcast_to(ar_ref[k], (nb, GB_ST))
        a_im = jnp.broadcast_to(ai_ref[k], (nb, GB_ST))
        s0 = carry_ref[be]
        sr, si = s0[:, :GB_ST], s0[:, GB_ST:]
        for t in range(tc):
            b = bu_ref[k, t * nb:(t + 1) * nb, :]
            sr, si = a_re * sr - a_im * si + b[:, :GB_ST], a_re * si + a_im * sr + b[:, GB_ST:]
            ss_ref[k, t * nb:(t + 1) * nb, :] = jnp.concatenate([sr, si], axis=-1)
        carry_ref[be] = jnp.concatenate([sr, si], axis=-1)
        str_ref[be] = sr
        sti_ref[be] = si

        y = jnp.dot(ss_ref[k].astype(BF16), cc_ref[k], preferred_element_type=F32)
        u32 = uhi_ref[be].astype(F32) + ulo_ref[be].astype(F32)
        y = y + d_ref[k] * u32
        g = jax.nn.gelu(y)
        gate = jnp.dot(g.astype(BF16), glu_ref[k], preferred_element_type=F32) + bglu_ref[k]
        yacc_ref[be] = g * jax.nn.sigmoid(gate)

    @pl.when(gi == N_GB // GB_PER_STEP - 1)
    def _():
        o = jnp.concatenate([yacc_ref[k] for k in range(N_GB)], axis=-1)
        on = _rms(o, gain_ref[...]).astype(BF16)
        if perm:
            on = jnp.dot(pt_ref[...], on, preferred_element_type=F32).astype(BF16)
            for b in range(n_seq):
                y_refs[b][...] = on[b * tc:(b + 1) * tc]
        else:
            y_refs[0][...] = on


def ssm_mixer(u_src, u_row_block, s0_re, s0_im, sp, *, n_seq, tc, nt, perm):
    nb = SCAN_ROWS
    rows = nb * tc
    n_u = n_seq if perm else 1
    blk_rows = tc if perm else rows
    in_specs = [pl.BlockSpec((blk_rows, SSM_WIDTH), functools.partial(lambda ti, be, b: (u_row_block(b, ti), 0), b=b))
                for b in range(n_u)]
    args = [u_src] * n_u
    if perm:
        pm = _perm_matrix(n_seq, tc)
        in_specs += [pl.BlockSpec(pm.shape, lambda ti, be: (0, 0)), pl.BlockSpec(pm.T.shape, lambda ti, be: (0, 0))]
        args += [jnp.asarray(pm, BF16), jnp.asarray(pm.T, BF16)]
    gps = GB_PER_STEP
    st_spec = pl.BlockSpec((nb, gps * GB_ST), lambda ti, gi: (0, gi))
    per_block = lambda r, c: pl.BlockSpec((gps, r, c), lambda ti, gi: (gi, 0, 0))
    in_specs += [
        st_spec, st_spec,
        per_block(1, GB_ST), per_block(1, GB_ST),
        per_block(GB_CH, 2 * GB_ST), per_block(2 * GB_ST, GB_CH), per_block(GB_CH, GB_CH),
        per_block(1, GB_CH), per_block(1, GB_CH),
        pl.BlockSpec((1, SSM_WIDTH), lambda ti, gi: (0, 0)),
    ]
    args += [s0_re, s0_im, sp['ar'], sp['ai'], sp['bb'], sp['cc'], sp['glu'], sp['d'], sp['bglu'], sp['gain']]
    y_shape = jax.ShapeDtypeStruct((nt * blk_rows, SSM_WIDTH), BF16)
    st_shape = jax.ShapeDtypeStruct((N_GB, nb, GB_ST), F32)
    st_out_spec = pl.BlockSpec((N_GB, nb, GB_ST), lambda ti, be: (0, 0, 0))
    out = pl.pallas_call(
        functools.partial(_ssm_kernel, n_seq=n_seq, tc=tc, perm=perm),
        out_shape=tuple([y_shape] * n_u + [st_shape, st_shape]),
        grid=(nt, N_GB // gps),
        in_specs=in_specs,
        out_specs=tuple([pl.BlockSpec((blk_rows, SSM_WIDTH), lambda ti, be: (ti, 0))] * n_u + [st_out_spec, st_out_spec]),
        scratch_shapes=[
            pltpu.VMEM((N_GB, rows, GB_CH), BF16),
            pltpu.VMEM((N_GB, rows, GB_CH), BF16),
            pltpu.VMEM((gps, rows, 2 * GB_ST), F32),
            pltpu.VMEM((gps, rows, 2 * GB_ST), F32),
            pltpu.VMEM((N_GB, rows, GB_CH), F32),
            pltpu.VMEM((N_GB, nb, 2 * GB_ST), F32),
        ],
        compiler_params=_cparams(("arbitrary", "arbitrary")),
        name="ssm_mixer",
    )(*args)
    flat = lambda s: jnp.swapaxes(s, 0, 1).reshape(nb, N_GB * GB_ST)
    return out[:n_u], flat(out[n_u]), flat(out[n_u + 1])


def _softplus(z):
    return jnp.maximum(z, 0.0) + jnp.log(1.0 + jnp.exp(-jnp.abs(z)))


def _split_bf16(x):
    hi = x.astype(BF16)
    return hi, (x - hi.astype(F32)).astype(BF16)


def _sb_block(q, kb, vb, upper, bias, run, acc, mask):
    z = lax.dot_general(q, kb.astype(BF16), (((1,), (1,)), ((), ())), preferred_element_type=F32)
    z = z * ATTN_SCALE + bias
    sp = _softplus(z)
    log_1m = -sp
    if mask is not None:
        log_1m = jnp.where(mask, log_1m, 0.0)
    hi, lo = _split_bf16(log_1m)
    after = (jnp.dot(hi, upper, preferred_element_type=F32)
             + jnp.dot(lo, upper, preferred_element_type=F32))
    a = jnp.exp((z - sp) + after + run)
    if mask is not None:
        a = jnp.where(mask, a, 0.0)
    acc = acc + jnp.dot(a.astype(BF16), vb.astype(BF16), preferred_element_type=F32)
    run = run + jnp.sum(log_1m, axis=1, keepdims=True)
    return run, acc


def _upper_matrix(n):
    j, s = np.meshgrid(np.arange(n), np.arange(n), indexing='ij')
    return (j > s).astype(np.float32)


def _upper_ext_matrix(n):
    return np.concatenate([_upper_matrix(n), np.ones((n, HEAD_DIM), np.float32)], axis=1)


LOG2E = 1.4426950408889634


def _sb_block_ext(q2, kb, vb, neg_upper_ext, bias2, run, mask):
    tk = kb.shape[0]
    w = lax.dot_general(q2, kb.astype(BF16), (((1,), (1,)), ((), ())), preferred_element_type=F32) + bias2
    sp2 = jnp.maximum(w, 0.0) + jnp.log2(1.0 + jnp.exp2(-jnp.abs(w)))
    if mask is not None:
        sp2 = jnp.where(mask, sp2, 0.0)
    ext = jnp.dot(sp2.astype(BF16), neg_upper_ext, preferred_element_type=F32)
    run_wide = jnp.concatenate([run] * (tk // HEAD_DIM), axis=1)
    a = jnp.exp2((w - sp2) + ext[:, :tk] + run_wide)
    if mask is not None:
        a = jnp.where(mask, a, 0.0)
    pv = jnp.dot(a.astype(BF16), vb.astype(BF16), preferred_element_type=F32)
    return run + ext[:, tk:], pv


HEADS_PER_STEP = 8


def _prompt_attn_kernel(bias_ref, q_ref, k_ref, v_ref, km_ref, vm_ref, up_ref, upm_ref, o_ref, run_ref, *, tb):
    hb = HEADS_PER_STEP
    hg = pl.program_id(1)
    iq = pl.program_id(2)
    heads = [slice(hh * HEAD_DIM, (hh + 1) * HEAD_DIM) for hh in range(hb)]
    bias = [bias_ref[hg * hb + hh] * LOG2E for hh in range(hb)]
    q = [(q_ref[:, sl] * (ATTN_SCALE * LOG2E)).astype(BF16) for sl in heads]
    row = lax.broadcasted_iota(jnp.int32, (tb, tb), 0)
    col = lax.broadcasted_iota(jnp.int32, (tb, tb), 1)

    d0 = pl.multiple_of(iq * tb, tb)
    for hh, sl in enumerate(heads):
        run, pv = _sb_block_ext(q[hh], k_ref[pl.ds(d0, tb), sl], v_ref[pl.ds(d0, tb), sl], up_ref[...], bias[hh],
                                jnp.zeros((tb, HEAD_DIM), F32), col < row)
        run_ref[hh] = run
        o_ref[:, sl] = pv

    def body(n, c):
        k0 = pl.multiple_of((iq - 1 - n) * tb, tb)
        for hh, sl in enumerate(heads):
            run, pv = _sb_block_ext(q[hh], k_ref[pl.ds(k0, tb), sl], v_ref[pl.ds(k0, tb), sl], up_ref[...], bias[hh],
                                    run_ref[hh], None)
            run_ref[hh] = run
            o_ref[:, sl] += pv
        return c

    lax.fori_loop(0, iq, body, 0)

    colm = lax.broadcasted_iota(jnp.int32, (tb, TAIL_ROWS), 1)
    for hh, sl in enumerate(heads):
        _, pv = _sb_block_ext(q[hh], km_ref[:, sl], vm_ref[:, sl], upm_ref[...], bias[hh], run_ref[hh],
                              colm >= META_OFF)
        o_ref[:, sl] += pv


def prompt_attention(proj, bias, seq=SEQ, n_batch=BATCH, tb=256):
    hb = HEADS_PER_STEP
    wb = hb * HEAD_DIM
    nqb = seq // tb
    qcol, kcol, vcol = SSM_WIDTH // wb, (SSM_WIDTH + ATTN_WIDTH) // wb, (SSM_WIDTH + 2 * ATTN_WIDTH) // wb
    tail_blk = (n_batch * seq) // TAIL_ROWS
    const = lambda a: pl.BlockSpec(a.shape, lambda b, h, i: (0, 0))
    up = jnp.asarray(-_upper_ext_matrix(tb), BF16)
    upm = jnp.asarray(-_upper_ext_matrix(TAIL_ROWS), BF16)
    return pl.pallas_call(
        functools.partial(_prompt_attn_kernel, tb=tb),
        out_shape=jax.ShapeDtypeStruct((n_batch * seq, ATTN_WIDTH), F32),
        grid=(n_batch, N_HEADS // hb, nqb),
        in_specs=[
            pl.BlockSpec(memory_space=pltpu.SMEM),
            pl.BlockSpec((tb, wb), lambda b, h, i: (b * nqb + i, qcol + h)),
            pl.BlockSpec((seq, wb), lambda b, h, i: (b, kcol + h)),
            pl.BlockSpec((seq, wb), lambda b, h, i: (b, vcol + h)),
            pl.BlockSpec((TAIL_ROWS, wb), lambda b, h, i: (tail_blk, kcol + h)),
            pl.BlockSpec((TAIL_ROWS, wb), lambda b, h, i: (tail_blk, vcol + h)),
            const(up), const(upm),
        ],
        out_specs=pl.BlockSpec((tb, wb), lambda b, h, i: (b * nqb + i, h)),
        scratch_shapes=[pltpu.VMEM((hb, tb, HEAD_DIM), F32)],
        compiler_params=_cparams(("parallel", "parallel", "arbitrary")),
        name="prompt_attention",
    )(bias, proj, proj, proj, proj, proj, up, upm)


def _meta_attn_kernel(bias_ref, q_ref, k_ref, v_ref, up_ref, o_ref):
    bias = bias_ref[pl.program_id(0)]
    row = lax.broadcasted_iota(jnp.int32, (TAIL_ROWS, TAIL_ROWS), 0)
    col = lax.broadcasted_iota(jnp.int32, (TAIL_ROWS, TAIL_ROWS), 1)
    mask = (col < row) & (col >= META_OFF)
    run = jnp.zeros((TAIL_ROWS, 1), F32)
    acc = jnp.zeros((TAIL_ROWS, HEAD_DIM), F32)
    _, acc = _sb_block(q_ref[...].astype(BF16), k_ref[...], v_ref[...], up_ref[...], bias, run, acc, mask)
    o_ref[...] = acc


def meta_attention(proj, bias, tail_blk=ROWS_MAIN // TAIL_ROWS):
    qcol, kcol, vcol = (SSM_WIDTH // HEAD_DIM, (SSM_WIDTH + ATTN_WIDTH) // HEAD_DIM,
                        (SSM_WIDTH + 2 * ATTN_WIDTH) // HEAD_DIM)
    blk = lambda c0: pl.BlockSpec((TAIL_ROWS, HEAD_DIM), lambda h: (tail_blk, c0 + h))
    return pl.pallas_call(
        _meta_attn_kernel,
        out_shape=jax.ShapeDtypeStruct((TAIL_ROWS, ATTN_WIDTH), F32),
        grid=(N_HEADS,),
        in_specs=[pl.BlockSpec(memory_space=pltpu.SMEM), blk(qcol), blk(kcol), blk(vcol),
                  pl.BlockSpec((TAIL_ROWS, TAIL_ROWS), lambda h: (0, 0))],
        out_specs=pl.BlockSpec((TAIL_ROWS, HEAD_DIM), lambda h: (0, h)),
        compiler_params=_cparams(("parallel",)),
        name="meta_attention",
    )(bias, proj, proj, proj, jnp.asarray(_upper_matrix(TAIL_ROWS), BF16))


PAGES_PER_STEP = 8


def _sample_attn_kernel(pt_ref, q_ref, *refs, n_steps):
    npg = PAGES_PER_STEP
    k_refs = refs[:npg]
    v_refs = refs[npg:2 * npg]
    g_ref, gt_ref, up_ref, bias_ref, o_ref, run_ref = refs[2 * npg:]
    jp = pl.program_id(1)
    nh = N_HEADS
    flat = PAGE_SIZE * nh

    @pl.when(jp == 0)
    def _():
        o_ref[...] = jnp.zeros_like(o_ref)
        run_ref[...] = jnp.zeros_like(run_ref)

    q = q_ref[...].astype(BF16)
    row = lax.broadcasted_iota(jnp.int32, (nh, flat), 0)
    col = lax.broadcasted_iota(jnp.int32, (nh, flat), 1)
    own = (col & (nh - 1)) == row

    zs = []
    for p in range(npg):
        kp = k_refs[p][...].reshape(flat, HEAD_DIM).astype(BF16)
        zt = lax.dot_general(q, kp, (((1,), (1,)), ((), ())), preferred_element_type=F32)
        zs.append(jnp.where(own, zt, 0.0))
    hi, lo = _split_bf16(jnp.concatenate(zs, axis=0))
    zc = jnp.dot(jnp.concatenate([hi, lo], axis=0), g_ref[...], preferred_element_type=F32)
    z = (zc[:npg * nh] + zc[npg * nh:]) * ATTN_SCALE + bias_ref[...]
    sp = _softplus(z)
    log_1m = -sp
    hi, lo = _split_bf16(log_1m)
    ext = jnp.dot(jnp.concatenate([hi, lo], axis=0), up_ref[...], preferred_element_type=F32)
    ext = ext[:npg * nh] + ext[npg * nh:]
    after, total = ext[:, :PAGE_SIZE], ext[:, PAGE_SIZE:]
    run = run_ref[...]
    runs = []
    for p in range(npg):
        runs.append(run)
        run = run + total[p * nh:(p + 1) * nh]
    run_ref[...] = run
    a = jnp.exp((z - sp) + after + jnp.concatenate(runs, axis=0))
    aw = jnp.dot(a.astype(BF16), gt_ref[...], preferred_element_type=F32)
    acc = o_ref[...]
    for p in range(npg):
        am = jnp.where(own, aw[p * nh:(p + 1) * nh], 0.0).astype(BF16)
        vp = v_refs[p][...].reshape(flat, HEAD_DIM).astype(BF16)
        acc = acc + jnp.dot(am, vp, preferred_element_type=F32)
    o_ref[...] = acc


def sample_attention(q, cache_k, cache_v, layer, page_table, bias):
    n_b, n_pages = page_table.shape
    npg = PAGES_PER_STEP
    n_steps = n_pages // npg
    flat = PAGE_SIZE * N_HEADS
    g = (np.arange(flat)[:, None] // N_HEADS == np.arange(PAGE_SIZE)[None, :]).astype(np.float32)
    bias_rows = jnp.tile(jnp.broadcast_to(bias[:, None], (N_HEADS, PAGE_SIZE)), (npg, 1))

    def page_spec(p):
        return pl.BlockSpec((None, None, PAGE_SIZE, N_HEADS, HEAD_DIM),
                            lambda b, jp, pt: (layer, pt[b, n_pages - 1 - (jp * npg + p)], 0, 0, 0))

    const = lambda shape: pl.BlockSpec(shape, lambda b, jp, pt: (0, 0))
    head_spec = pl.BlockSpec((None, N_HEADS, HEAD_DIM), lambda b, jp, pt: (b, 0, 0))
    grid_spec = pltpu.PrefetchScalarGridSpec(
        num_scalar_prefetch=1,
        grid=(n_b, n_steps),
        in_specs=([head_spec] + [page_spec(p) for p in range(npg)] * 2
                  + [const((flat, PAGE_SIZE)), const((PAGE_SIZE, flat)),
                     const((PAGE_SIZE, PAGE_SIZE + HEAD_DIM)), const((npg * N_HEADS, PAGE_SIZE))]),
        out_specs=head_spec,
        scratch_shapes=[pltpu.VMEM((N_HEADS, HEAD_DIM), F32)],
    )
    return pl.pallas_call(
        functools.partial(_sample_attn_kernel, n_steps=n_steps),
        out_shape=jax.ShapeDtypeStruct((n_b, N_HEADS, HEAD_DIM), F32),
        grid_spec=grid_spec,
        compiler_params=_cparams(("arbitrary", "arbitrary")),
        name="sample_attention",
    )(page_table, q, *([cache_k] * npg), *([cache_v] * npg),
      jnp.asarray(g, BF16), jnp.asarray(g.T, BF16), jnp.asarray(_upper_ext_matrix(PAGE_SIZE), BF16), bias_rows)


def _ssm_params(i, lam_re, lam_im, log_step, b_re, b_im, c_re, c_im, d, w_glu, b_glu, out_gain):
    ar, ai, bbr, bbi = ssm_discretize(lam_re[i], lam_im[i], log_step[i],
                                      jnp.swapaxes(b_re[i], 1, 2), jnp.swapaxes(b_im[i], 1, 2))
    bb = jnp.concatenate([_block_diag(bbr), _block_diag(bbi)], axis=-1).astype(BF16)
    cc = jnp.concatenate([_block_diag(jnp.swapaxes(c_re[i], 1, 2)),
                          -_block_diag(jnp.swapaxes(c_im[i], 1, 2))], axis=1).astype(BF16)
    return dict(
        ar=ar.reshape(N_GB, 1, GB_ST), ai=ai.reshape(N_GB, 1, GB_ST), bb=bb, cc=cc,
        glu=_block_diag(w_glu[i]).astype(BF16),
        d=d[i].reshape(N_GB, 1, GB_CH), bglu=b_glu[i].reshape(N_GB, 1, GB_CH),
        gain=out_gain[i].reshape(1, SSM_WIDTH))


def kernel(x_prompt, x_sample, cache_k, cache_v, state_ssm_re, state_ssm_im, page_table, meta_tokens,
           ffn1_norm_pre, ffn1_norm_post, ffn1_w_gate, ffn1_w_up, ffn1_w_down,
           mix_norm_pre, mix_norm_post, w_in, sb_logit_bias,
           ssm_lambda_re, ssm_lambda_im, ssm_log_step, ssm_b_re, ssm_b_im, ssm_c_re, ssm_c_im,
           ssm_d, ssm_w_glu, ssm_b_glu, ssm_out_norm, attn_out_norm, w_out,
           ffn2_norm_pre, ffn2_norm_post, ffn2_w_gate, ffn2_w_up, ffn2_w_down):
    n_pad = TAIL_ROWS - DEC_BATCH - N_META
    tail_of = lambda sample, meta: jnp.concatenate([sample, jnp.zeros((n_pad, sample.shape[1]), sample.dtype), meta], 0)
    x = (x_prompt.reshape(ROWS_MAIN, D_MODEL), tail_of(x_sample.reshape(DEC_BATCH, D_MODEL), meta_tokens.astype(F32)))
    zero_state = jnp.zeros((SCAN_ROWS, SSM_GROUPS * SSM_STATE), F32)
    tail_blk16 = (ROWS_MAIN + META_OFF) // N_META
    kcol, vcol = SSM_WIDTH + ATTN_WIDTH, SSM_WIDTH + 2 * ATTN_WIDTH

    def ffn(x, h, w_bf16, casts, g_post, g_next):
        if w_bf16[2] is None:
            a, (w_down, *cast) = gateup(h, w_bf16[0], w_bf16[1], [w_bf16[3]] + casts)
        else:
            (a, cast), w_down = gateup(h, w_bf16[0], w_bf16[1], casts), w_bf16[2]
        f = matmul(a, w_down, tn=512)
        if g_next is None:
            return (resnorm_last(f, x, g_post, 0.5, 0, ROWS_MAIN),
                    resnorm_last(f, x, g_post, 0.5, ROWS_MAIN, TAIL_ROWS)), None, cast
        return resnorm(f, x, g_post, g_next, 0.5) + (cast,)

    ffn_casts = lambda wg, wu, wd, i: [(wg, i, True), (wu, i, True), (wd, i, False)]

    outs = {k: [] for k in ('kp', 'vp', 'srp', 'sip', 'ks', 'vs', 'srs', 'sis')}
    h = rmsnorm_rows(x[0], x[1], ffn1_norm_pre[0])
    w_ffn1 = [cast_weight(ffn1_w_gate, 0), cast_weight(ffn1_w_up, 0), None, (ffn1_w_down, 0, False)]
    for i in range(DEPTH):
        casts = [(w_in, i, False), (w_out, i, False)] + ffn_casts(ffn2_w_gate, ffn2_w_up, ffn2_w_down, i)
        x, h, (w_in_b, w_out_b, *w_ffn2) = ffn(x, h, w_ffn1, casts, ffn1_norm_post[i], mix_norm_pre[i])

        proj = matmul(h, w_in_b, tm=TM_BIG, tn=512)
        sp = _ssm_params(i, ssm_lambda_re, ssm_lambda_im, ssm_log_step, ssm_b_re, ssm_b_im,
                         ssm_c_re, ssm_c_im, ssm_d, ssm_w_glu, ssm_b_glu, ssm_out_norm)
        ys_meta, mr, mi = ssm_mixer(proj, lambda b, ti: tail_blk16, zero_state, zero_state, sp,
                                    n_seq=BATCH, tc=N_META, nt=1, perm=True)
        ys_main, pr, pi = ssm_mixer(proj, lambda b, ti: b * (SEQ // SCAN_TC) + ti, mr, mi, sp,
                                    n_seq=BATCH, tc=SCAN_TC, nt=SEQ // SCAN_TC, perm=True)
        ys_samp, sr, si = ssm_mixer(proj, lambda b, ti: ROWS_MAIN // DEC_BATCH,
                                    state_ssm_re[i].reshape(DEC_BATCH, -1), state_ssm_im[i].reshape(DEC_BATCH, -1),
                                    sp, n_seq=DEC_BATCH, tc=1, nt=1, perm=False)
        pr, pi = pr[:BATCH], pi[:BATCH]
        ys = jnp.concatenate(list(ys_main) + [ys_samp[0], jnp.zeros((n_pad, SSM_WIDTH), BF16), ys_meta[0]], axis=0)

        at_main = prompt_attention(proj, sb_logit_bias[i])
        at_meta = meta_attention(proj, sb_logit_bias[i])
        q_samp = proj[ROWS_MAIN:ROWS_MAIN + DEC_BATCH, SSM_WIDTH:SSM_WIDTH + ATTN_WIDTH]
        at_samp = sample_attention(q_samp.reshape(DEC_BATCH, N_HEADS, HEAD_DIM), cache_k, cache_v, i, page_table,
                                   sb_logit_bias[i]).reshape(DEC_BATCH, ATTN_WIDTH)
        ya = rmsnorm_rows(at_main, tail_of(at_samp, at_meta[META_OFF:]), attn_out_norm[i])

        f = matmul2(ys, ya, w_out_b)
        x, h = resnorm(f, x, mix_norm_post[i], ffn2_norm_pre[i], 1.0)

        last = i + 1 == DEPTH
        casts = [] if last else ffn_casts(ffn1_w_gate, ffn1_w_up, ffn1_w_down, i + 1)
        x, h, w_ffn1 = ffn(x, h, w_ffn2, casts, ffn2_norm_post[i], None if last else ffn1_norm_pre[i + 1])

        k_all, v_all = proj[:, kcol:kcol + ATTN_WIDTH], proj[:, vcol:vcol + ATTN_WIDTH]

        def prompt_rows(t):
            meta = jnp.broadcast_to(t[ROWS_MAIN + META_OFF:][None], (BATCH, N_META, ATTN_WIDTH))
            full = jnp.concatenate([meta, t[:ROWS_MAIN].reshape(BATCH, SEQ, ATTN_WIDTH)], axis=1)
            return full.reshape(BATCH, N_META + SEQ, N_HEADS, HEAD_DIM)

        sample_rows = lambda t: t[ROWS_MAIN:ROWS_MAIN + DEC_BATCH].reshape(DEC_BATCH, 1, N_HEADS, HEAD_DIM)
        outs['kp'].append(prompt_rows(k_all))
        outs['vp'].append(prompt_rows(v_all))
        outs['ks'].append(sample_rows(k_all))
        outs['vs'].append(sample_rows(v_all))
        outs['srp'].append(pr.reshape(BATCH, SSM_GROUPS, SSM_STATE))
        outs['sip'].append(pi.reshape(BATCH, SSM_GROUPS, SSM_STATE))
        outs['srs'].append(sr.reshape(DEC_BATCH, SSM_GROUPS, SSM_STATE))
        outs['sis'].append(si.reshape(DEC_BATCH, SSM_GROUPS, SSM_STATE))

    x_main, x_tail = x
    y_prompt = x_main.reshape(BATCH, SEQ, D_MODEL)
    y_sample = x_tail[:DEC_BATCH].reshape(DEC_BATCH, 1, D_MODEL)
    st = lambda k: jnp.stack(outs[k])
    return (y_prompt, y_sample, st('kp'), st('vp'), st('srp'), st('sip'),
            st('ks'), st('vs'), st('srs'), st('sis'))
```

```python
import functools
import math

import numpy as np
import jax
import jax.numpy as jnp
from jax import lax
from jax.experimental import pallas as pl
from jax.experimental.pallas import tpu as pltpu

F32 = jnp.float32
BF16 = jnp.bfloat16

D_MODEL = 4096
BATCH = 4
SEQ = 2048
DEPTH = 2
DEC_BATCH = 8
PAGE_SIZE = 128
N_HEADS = 16
HEAD_DIM = 128
ATTN_WIDTH = N_HEADS * HEAD_DIM
SSM_CH = 16
SSM_GROUPS = 128
SSM_STATE = 64
SSM_WIDTH = SSM_GROUPS * SSM_CH
D_FF = 11008
N_META = 16
RMS_EPS = 1e-6
LAMBDA_RE_MAX = -1e-4
ATTN_SCALE = 1.0 / math.sqrt(HEAD_DIM)

ROWS_MAIN = BATCH * SEQ
TAIL_ROWS = 128
ROWS = ROWS_MAIN + TAIL_ROWS
META_OFF = TAIL_ROWS - N_META

GB = 16
N_GB = SSM_GROUPS // GB
GB_CH = GB * SSM_CH
GB_ST = GB * SSM_STATE

TM = 640
VMEM_LIMIT = 56 * 1024 * 1024


def _cparams(sem, vmem=VMEM_LIMIT):
    return pltpu.CompilerParams(dimension_semantics=sem, vmem_limit_bytes=vmem)


def _rms(x, gain):
    r = lax.rsqrt(jnp.mean(x * x, axis=-1, keepdims=True) + RMS_EPS)
    return x * r * gain


def _rows_of(main_ref, tail_ref):
    n_main = pl.num_programs(0) - 1
    return jnp.where(pl.program_id(0) < n_main, main_ref[...], tail_ref[...])


def _split_row_specs(main, tm):
    d = main.shape[1]
    last = main.shape[0] // tm - 1
    return [pl.BlockSpec((tm, d), lambda i: (jnp.minimum(i, last), 0)), pl.BlockSpec((tm, d), lambda i: (0, 0))]


def _rmsnorm_kernel(xm_ref, xt_ref, g_ref, o_ref):
    o_ref[...] = _rms(_rows_of(xm_ref, xt_ref), g_ref[...]).astype(o_ref.dtype)


def rmsnorm_rows(x_main, x_tail, gain, out_dtype=BF16):
    tm = TAIL_ROWS
    rows, d = x_main.shape[0] + tm, x_main.shape[1]
    return pl.pallas_call(
        _rmsnorm_kernel,
        out_shape=jax.ShapeDtypeStruct((rows, d), out_dtype),
        grid=(rows // tm,),
        in_specs=_split_row_specs(x_main, tm) + [pl.BlockSpec((1, d), lambda i: (0, 0))],
        out_specs=pl.BlockSpec((tm, d), lambda i: (i, 0)),
        compiler_params=_cparams(("arbitrary",)),
        name="rmsnorm_rows",
    )(x_main, x_tail, gain.reshape(1, d))


def _round_up(x, m):
    return -(-x // m) * m


def _cast_tiling(shape, n_row_blocks, n_col_blocks):
    r, c = shape
    return _round_up(-(-r // n_row_blocks), 16), _round_up(-(-c // n_col_blocks), HEAD_DIM)


def _gateup_kernel(h_ref, wg_ref, wu_ref, *refs, n_cast):
    cast_in, o_ref, cast_out = refs[:n_cast], refs[n_cast], refs[n_cast + 1:]
    h = h_ref[...]
    g = jnp.dot(h, wg_ref[...], preferred_element_type=F32)
    u = jnp.dot(h, wu_ref[...], preferred_element_type=F32)
    o_ref[...] = (g * jax.nn.sigmoid(g) * u).astype(o_ref.dtype)
    for src, dst in zip(cast_in, cast_out):
        dst[...] = src[...].astype(BF16)


TM_BIG = 1664


def gateup(h, wg, wu, casts=(), tm=TM_BIG, tf=256):
    rows, d = h.shape
    f = wg.shape[1]
    grid = (rows // tm, f // tf)
    w_spec = pl.BlockSpec((d, tf), lambda i, j: (0, j))
    cast_in, cast_out, cast_shapes = [], [], []
    for w, layer, row_major in casts:
        r, c = w.shape[1:]
        if row_major:
            br, bc = _cast_tiling((r, c), grid[0], grid[1])
            imap = lambda i, j: (i, j)
        else:
            br, bc = _cast_tiling((r, c), grid[1], grid[0])
            imap = lambda i, j: (j, i)
        cast_in.append(pl.BlockSpec((None, br, bc), functools.partial(lambda i, j, m, l: (l,) + m(i, j), m=imap, l=layer)))
        cast_out.append(pl.BlockSpec((br, bc), imap))
        cast_shapes.append(jax.ShapeDtypeStruct((r, c), BF16))
    out = pl.pallas_call(
        functools.partial(_gateup_kernel, n_cast=len(casts)),
        out_shape=tuple([jax.ShapeDtypeStruct((rows, f), BF16)] + cast_shapes),
        grid=grid,
        in_specs=[pl.BlockSpec((tm, d), lambda i, j: (i, 0)), w_spec, w_spec] + cast_in,
        out_specs=tuple([pl.BlockSpec((tm, tf), lambda i, j: (i, j))] + cast_out),
        compiler_params=_cparams(("arbitrary", "arbitrary")),
        name="ffn_gateup",
    )(h, wg, wu, *[w for w, _, _ in casts])
    return out[0], list(out[1:])


def _cast_kernel(w_ref, o_ref):
    o_ref[...] = w_ref[...].astype(o_ref.dtype)


def cast_weight(w, layer, tr=256):
    r, c = w.shape[1:]
    return pl.pallas_call(
        _cast_kernel,
        out_shape=jax.ShapeDtypeStruct((r, c), BF16),
        grid=(r // tr,),
        in_specs=[pl.BlockSpec((None, tr, c), lambda i: (layer, i, 0))],
        out_specs=pl.BlockSpec((tr, c), lambda i: (i, 0)),
        compiler_params=_cparams(("parallel",)),
        name="cast_weight",
    )(w)


def _mm_kernel(a_ref, w_ref, o_ref):
    o_ref[...] = jnp.dot(a_ref[...], w_ref[...], preferred_element_type=F32).astype(o_ref.dtype)


def matmul(a, w, tm=TM, tn=256, out_dtype=F32):
    rows, k = a.shape
    n = w.shape[1]
    return pl.pallas_call(
        _mm_kernel,
        out_shape=jax.ShapeDtypeStruct((rows, n), out_dtype),
        grid=(rows // tm, n // tn),
        in_specs=[pl.BlockSpec((tm, k), lambda i, j: (i, 0)),
                  pl.BlockSpec((k, tn), lambda i, j: (0, j))],
        out_specs=pl.BlockSpec((tm, tn), lambda i, j: (i, j)),
        compiler_params=_cparams(("parallel", "arbitrary")),
        name="matmul",
    )(a, w)


def _mm2_kernel(a1_ref, a2_ref, w1_ref, w2_ref, o_ref):
    acc = jnp.dot(a1_ref[...], w1_ref[...], preferred_element_type=F32)
    acc = acc + jnp.dot(a2_ref[...], w2_ref[...], preferred_element_type=F32)
    o_ref[...] = acc


def matmul2(a1, a2, w, tm=TM_BIG, tn=512):
    rows, k1 = a1.shape
    k2 = a2.shape[1]
    assert k1 == k2 and w.shape[0] == k1 + k2
    n = w.shape[1]
    return pl.pallas_call(
        _mm2_kernel,
        out_shape=jax.ShapeDtypeStruct((rows, n), F32),
        grid=(rows // tm, n // tn),
        in_specs=[pl.BlockSpec((tm, k1), lambda i, j: (i, 0)),
                  pl.BlockSpec((tm, k2), lambda i, j: (i, 0)),
                  pl.BlockSpec((k1, tn), lambda i, j: (0, j)),
                  pl.BlockSpec((k2, tn), lambda i, j: (1, j))],
        out_specs=pl.BlockSpec((tm, tn), lambda i, j: (i, j)),
        compiler_params=_cparams(("parallel", "arbitrary")),
        name="matmul2",
    )(a1, a2, w, w)


def _resnorm_kernel(f_ref, *refs, scale, split):
    if split:
        xm_ref, xt_ref, gp_ref, gn_ref, xo_ref, ho_ref = refs
        x = _rows_of(xm_ref, xt_ref)
    else:
        x_ref, gp_ref, gn_ref, xo_ref, ho_ref = refs
        x = x_ref[...]
    xn = x + scale * _rms(f_ref[...], gp_ref[...])
    xo_ref[...] = xn
    ho_ref[...] = _rms(xn, gn_ref[...]).astype(ho_ref.dtype)


def resnorm(f, x, gain_post, gain_next, scale):
    tm = TAIL_ROWS
    rows, d = f.shape
    row_spec = pl.BlockSpec((tm, d), lambda i: (i, 0))
    vec_spec = pl.BlockSpec((1, d), lambda i: (0, 0))
    split = isinstance(x, tuple)
    x_specs, x_args = (_split_row_specs(x[0], tm), list(x)) if split else ([row_spec], [x])
    return pl.pallas_call(
        functools.partial(_resnorm_kernel, scale=scale, split=split),
        out_shape=(jax.ShapeDtypeStruct((rows, d), F32), jax.ShapeDtypeStruct((rows, d), BF16)),
        grid=(rows // tm,),
        in_specs=[row_spec] + x_specs + [vec_spec, vec_spec],
        out_specs=(row_spec, row_spec),
        compiler_params=_cparams(("arbitrary",)),
        name="resnorm",
    )(f, *x_args, gain_post.reshape(1, d), gain_next.reshape(1, d))


def _resnorm_last_kernel(f_ref, x_ref, gp_ref, xo_ref, *, scale):
    xo_ref[...] = x_ref[...] + scale * _rms(f_ref[...], gp_ref[...])


def resnorm_last(f, x, gain_post, scale, row0, n_rows, tm=128):
    d = x.shape[1]
    b0 = row0 // tm
    row_spec = pl.BlockSpec((tm, d), lambda i: (b0 + i, 0))
    return pl.pallas_call(
        functools.partial(_resnorm_last_kernel, scale=scale),
        out_shape=jax.ShapeDtypeStruct((n_rows, d), F32),
        grid=(n_rows // tm,),
        in_specs=[row_spec, row_spec, pl.BlockSpec((1, d), lambda i: (0, 0))],
        out_specs=pl.BlockSpec((tm, d), lambda i: (i, 0)),
        compiler_params=_cparams(("parallel",)),
        name="resnorm_last",
    )(f, x, gain_post.reshape(1, d))


def _ssm_discretize_kernel(lre_ref, lim_ref, ls_ref, bre_ref, bim_ref,
                           ar_ref, ai_ref, bbr_ref, bbi_ref):
    step = jnp.exp(ls_ref[...])
    lr = jnp.minimum(lre_ref[...], LAMBDA_RE_MAX)
    li = lim_ref[...]
    mag = jnp.exp(lr * step)
    ar = mag * jnp.cos(li * step)
    ai = mag * jnp.sin(li * step)
    den = lr * lr + li * li
    fr = ((ar - 1.0) * lr + ai * li) / den
    fi = (ai * lr - (ar - 1.0) * li) / den
    br = bre_ref[...]
    bi = bim_ref[...]
    ar_ref[...] = ar
    ai_ref[...] = ai
    bbr_ref[...] = fr * br - fi * bi
    bbi_ref[...] = fr * bi + fi * br


def ssm_discretize(lam_re, lam_im, log_step, b_re, b_im):
    g, p = lam_re.shape
    ch = b_re.shape[1]
    gp = jax.ShapeDtypeStruct((g, 1, p), F32)
    gcp = jax.ShapeDtypeStruct((g, ch, p), F32)
    return pl.pallas_call(
        _ssm_discretize_kernel,
        out_shape=(gp, gp, gcp, gcp),
        name="ssm_discretize",
    )(lam_re.reshape(g, 1, p), lam_im.reshape(g, 1, p), log_step.reshape(g, 1, 1), b_re, b_im)


def _block_diag(x):
    _, r, c = x.shape
    tiled = jnp.tile(x.reshape(N_GB, GB * r, c), (1, 1, GB))
    on_diag = (np.arange(GB * r)[:, None] // r) == (np.arange(GB * c)[None, :] // c)
    return jnp.where(on_diag[None], tiled, jnp.zeros((), x.dtype))


SCAN_ROWS = 8
SCAN_TC = 64
GB_PER_STEP = 2


def _perm_matrix(n_seq, tc):
    p = np.zeros((SCAN_ROWS * tc, n_seq * tc), np.float32)
    t, b = np.meshgrid(np.arange(tc), np.arange(n_seq), indexing='ij')
    p[(t * SCAN_ROWS + b).ravel(), (b * tc + t).ravel()] = 1.0
    return p


def _ssm_kernel(*refs, n_seq, tc, perm):
    nb = SCAN_ROWS
    n_u = n_seq if perm else 1
    u_refs = refs[:n_u]
    i = n_u
    if perm:
        p_ref, pt_ref = refs[i], refs[i + 1]
        i += 2
    (s0r_ref, s0i_ref, ar_ref, ai_ref, bb_ref, cc_ref, glu_ref, d_ref, bglu_ref, gain_ref) = refs[i:i + 10]
    i += 10
    y_refs = refs[i:i + n_u]
    i += n_u
    str_ref, sti_ref = refs[i], refs[i + 1]
    uhi_ref, ulo_ref, bu_ref, ss_ref, yacc_ref, carry_ref = refs[i + 2:]

    ti = pl.program_id(0)
    gi = pl.program_id(1)

    @pl.when(gi == 0)
    def _():
        if perm:
            ucat = jnp.concatenate([r[...] for r in u_refs], axis=0)
        else:
            ucat = u_refs[0][...]
        hi = ucat.astype(BF16)
        lo = (ucat - hi.astype(F32)).astype(BF16)
        if perm:
            hi = jnp.dot(p_ref[...], hi, preferred_element_type=F32).astype(BF16)
            lo = jnp.dot(p_ref[...], lo, preferred_element_type=F32).astype(BF16)
        for k in range(N_GB):
            uhi_ref[k] = hi[:, k * GB_CH:(k + 1) * GB_CH]
            ulo_ref[k] = lo[:, k * GB_CH:(k + 1) * GB_CH]

    @pl.when(ti == 0)
    def _():
        for k in range(GB_PER_STEP):
            sl = slice(k * GB_ST, (k + 1) * GB_ST)
            carry_ref[gi * GB_PER_STEP + k] = jnp.concatenate([s0r_ref[:, sl], s0i_ref[:, sl]], axis=-1)

    for k in range(GB_PER_STEP):
        bu_ref[k] = jnp.dot(uhi_ref[gi * GB_PER_STEP + k], bb_ref[k], preferred_element_type=F32)

    for k in range(GB_PER_STEP):
        be = gi * GB_PER_STEP + k
        a_re = jnp.broadcast_to(ar_ref[k], (nb, GB_ST))
        a_im = jnp.broadcast_to(ai_ref[k], (nb, GB_ST))
        s0 = carry_ref[be]
        sr, si = s0[:, :GB_ST], s0[:, GB_ST:]
        for t in range(tc):
            b = bu_ref[k, t * nb:(t + 1) * nb, :]
            sr, si = a_re * sr - a_im * si + b[:, :GB_ST], a_re * si + a_im * sr + b[:, GB_ST:]
            ss_ref[k, t * nb:(t + 1) * nb, :] = jnp.concatenate([sr, si], axis=-1)
        carry_ref[be] = jnp.concatenate([sr, si], axis=-1)
        str_ref[be] = sr
        sti_ref[be] = si

        y = jnp.dot(ss_ref[k].astype(BF16), cc_ref[k], preferred_element_type=F32)
        u32 = uhi_ref[be].astype(F32) + ulo_ref[be].astype(F32)
        y = y + d_ref[k] * u32
        g = jax.nn.gelu(y)
        gate = jnp.dot(g.astype(BF16), glu_ref[k], preferred_element_type=F32) + bglu_ref[k]
        yacc_ref[be] = g * jax.nn.sigmoid(gate)

    @pl.when(gi == N_GB // GB_PER_STEP - 1)
    def _():
        o = jnp.concatenate([yacc_ref[k] for k in range(N_GB)], axis=-1)
        on = _rms(o, gain_ref[...]).astype(BF16)
        if perm:
            on = jnp.dot(pt_ref[...], on, preferred_element_type=F32).astype(BF16)
            for b in range(n_seq):
                y_refs[b][...] = on[b * tc:(b + 1) * tc]
        else:
            y_refs[0][...] = on


def ssm_mixer(u_src, u_row_block, s0_re, s0_im, sp, *, n_seq, tc, nt, perm):
    nb = SCAN_ROWS
    rows = nb * tc
    n_u = n_seq if perm else 1
    blk_rows = tc if perm else rows
    in_specs = [pl.BlockSpec((blk_rows, SSM_WIDTH), functools.partial(lambda ti, be, b: (u_row_block(b, ti), 0), b=b))
                for b in range(n_u)]
    args = [u_src] * n_u
    if perm:
        pm = _perm_matrix(n_seq, tc)
        in_specs += [pl.BlockSpec(pm.shape, lambda ti, be: (0, 0)), pl.BlockSpec(pm.T.shape, lambda ti, be: (0, 0))]
        args += [jnp.asarray(pm, BF16), jnp.asarray(pm.T, BF16)]
    gps = GB_PER_STEP
    st_spec = pl.BlockSpec((nb, gps * GB_ST), lambda ti, gi: (0, gi))
    per_block = lambda r, c: pl.BlockSpec((gps, r, c), lambda ti, gi: (gi, 0, 0))
    in_specs += [
        st_spec, st_spec,
        per_block(1, GB_ST), per_block(1, GB_ST),
        per_block(GB_CH, 2 * GB_ST), per_block(2 * GB_ST, GB_CH), per_block(GB_CH, GB_CH),
        per_block(1, GB_CH), per_block(1, GB_CH),
        pl.BlockSpec((1, SSM_WIDTH), lambda ti, gi: (0, 0)),
    ]
    args += [s0_re, s0_im, sp['ar'], sp['ai'], sp['bb'], sp['cc'], sp['glu'], sp['d'], sp['bglu'], sp['gain']]
    y_shape = jax.ShapeDtypeStruct((nt * blk_rows, SSM_WIDTH), BF16)
    st_shape = jax.ShapeDtypeStruct((N_GB, nb, GB_ST), F32)
    st_out_spec = pl.BlockSpec((N_GB, nb, GB_ST), lambda ti, be: (0, 0, 0))
    out = pl.pallas_call(
        functools.partial(_ssm_kernel, n_seq=n_seq, tc=tc, perm=perm),
        out_shape=tuple([y_shape] * n_u + [st_shape, st_shape]),
        grid=(nt, N_GB // gps),
        in_specs=in_specs,
        out_specs=tuple([pl.BlockSpec((blk_rows, SSM_WIDTH), lambda ti, be: (ti, 0))] * n_u + [st_out_spec, st_out_spec]),
        scratch_shapes=[
            pltpu.VMEM((N_GB, rows, GB_CH), BF16),
            pltpu.VMEM((N_GB, rows, GB_CH), BF16),
            pltpu.VMEM((gps, rows, 2 * GB_ST), F32),
            pltpu.VMEM((gps, rows, 2 * GB_ST), F32),
            pltpu.VMEM((N_GB, rows, GB_CH), F32),
            pltpu.VMEM((N_GB, nb, 2 * GB_ST), F32),
        ],
        compiler_params=_cparams(("arbitrary", "arbitrary")),
        name="ssm_mixer",
    )(*args)
    flat = lambda s: jnp.swapaxes(s, 0, 1).reshape(nb, N_GB * GB_ST)
    return out[:n_u], flat(out[n_u]), flat(out[n_u + 1])


def _softplus(z):
    return jnp.maximum(z, 0.0) + jnp.log(1.0 + jnp.exp(-jnp.abs(z)))


def _split_bf16(x):
    hi = x.astype(BF16)
    return hi, (x - hi.astype(F32)).astype(BF16)


def _sb_block(q, kb, vb, upper, bias, run, acc, mask):
    z = lax.dot_general(q, kb.astype(BF16), (((1,), (1,)), ((), ())), preferred_element_type=F32)
    z = z * ATTN_SCALE + bias
    sp = _softplus(z)
    log_1m = -sp
    if mask is not None:
        log_1m = jnp.where(mask, log_1m, 0.0)
    hi, lo = _split_bf16(log_1m)
    after = (jnp.dot(hi, upper, preferred_element_type=F32)
             + jnp.dot(lo, upper, preferred_element_type=F32))
    a = jnp.exp((z - sp) + after + run)
    if mask is not None:
        a = jnp.where(mask, a, 0.0)
    acc = acc + jnp.dot(a.astype(BF16), vb.astype(BF16), preferred_element_type=F32)
    run = run + jnp.sum(log_1m, axis=1, keepdims=True)
    return run, acc


def _upper_matrix(n):
    j, s = np.meshgrid(np.arange(n), np.arange(n), indexing='ij')
    return (j > s).astype(np.float32)


def _upper_ext_matrix(n):
    return np.concatenate([_upper_matrix(n), np.ones((n, HEAD_DIM), np.float32)], axis=1)


LOG2E = 1.4426950408889634


def _sb_block_ext(q2, kb, vb, neg_upper_ext, bias2, run, mask):
    tk = kb.shape[0]
    w = lax.dot_general(q2, kb.astype(BF16), (((1,), (1,)), ((), ())), preferred_element_type=F32) + bias2
    sp2 = jnp.maximum(w, 0.0) + jnp.log2(1.0 + jnp.exp2(-jnp.abs(w)))
    if mask is not None:
        sp2 = jnp.where(mask, sp2, 0.0)
    ext = jnp.dot(sp2.astype(BF16), neg_upper_ext, preferred_element_type=F32)
    run_wide = jnp.concatenate([run] * (tk // HEAD_DIM), axis=1)
    a = jnp.exp2((w - sp2) + ext[:, :tk] + run_wide)
    if mask is not None:
        a = jnp.where(mask, a, 0.0)
    pv = jnp.dot(a.astype(BF16), vb.astype(BF16), preferred_element_type=F32)
    return run + ext[:, tk:], pv


HEADS_PER_STEP = 8


def _prompt_attn_kernel(bias_ref, q_ref, k_ref, v_ref, km_ref, vm_ref, up_ref, upm_ref, o_ref, run_ref, *, tb):
    hb = HEADS_PER_STEP
    hg = pl.program_id(1)
    iq = pl.program_id(2)
    heads = [slice(hh * HEAD_DIM, (hh + 1) * HEAD_DIM) for hh in range(hb)]
    bias = [bias_ref[hg * hb + hh] * LOG2E for hh in range(hb)]
    q = [(q_ref[:, sl] * (ATTN_SCALE * LOG2E)).astype(BF16) for sl in heads]
    row = lax.broadcasted_iota(jnp.int32, (tb, tb), 0)
    col = lax.broadcasted_iota(jnp.int32, (tb, tb), 1)

    d0 = pl.multiple_of(iq * tb, tb)
    for hh, sl in enumerate(heads):
        run, pv = _sb_block_ext(q[hh], k_ref[pl.ds(d0, tb), sl], v_ref[pl.ds(d0, tb), sl], up_ref[...], bias[hh],
                                jnp.zeros((tb, HEAD_DIM), F32), col < row)
        run_ref[hh] = run
        o_ref[:, sl] = pv

    def body(n, c):
        k0 = pl.multiple_of((iq - 1 - n) * tb, tb)
        for hh, sl in enumerate(heads):
            run, pv = _sb_block_ext(q[hh], k_ref[pl.ds(k0, tb), sl], v_ref[pl.ds(k0, tb), sl], up_ref[...], bias[hh],
                                    run_ref[hh], None)
            run_ref[hh] = run
            o_ref[:, sl] += pv
        return c

    lax.fori_loop(0, iq, body, 0)

    colm = lax.broadcasted_iota(jnp.int32, (tb, TAIL_ROWS), 1)
    for hh, sl in enumerate(heads):
        _, pv = _sb_block_ext(q[hh], km_ref[:, sl], vm_ref[:, sl], upm_ref[...], bias[hh], run_ref[hh],
                              colm >= META_OFF)
        o_ref[:, sl] += pv


def prompt_attention(proj, bias, seq=SEQ, n_batch=BATCH, tb=256):
    hb = HEADS_PER_STEP
    wb = hb * HEAD_DIM
    nqb = seq // tb
    qcol, kcol, vcol = SSM_WIDTH // wb, (SSM_WIDTH + ATTN_WIDTH) // wb, (SSM_WIDTH + 2 * ATTN_WIDTH) // wb
    tail_blk = (n_batch * seq) // TAIL_ROWS
    const = lambda a: pl.BlockSpec(a.shape, lambda b, h, i: (0, 0))
    up = jnp.asarray(-_upper_ext_matrix(tb), BF16)
    upm = jnp.asarray(-_upper_ext_matrix(TAIL_ROWS), BF16)
    return pl.pallas_call(
        functools.partial(_prompt_attn_kernel, tb=tb),
        out_shape=jax.ShapeDtypeStruct((n_batch * seq, ATTN_WIDTH), F32),
        grid=(n_batch, N_HEADS // hb, nqb),
        in_specs=[
            pl.BlockSpec(memory_space=pltpu.SMEM),
            pl.BlockSpec((tb, wb), lambda b, h, i: (b * nqb + i, qcol + h)),
            pl.BlockSpec((seq, wb), lambda b, h, i: (b, kcol + h)),
            pl.BlockSpec((seq, wb), lambda b, h, i: (b, vcol + h)),
            pl.BlockSpec((TAIL_ROWS, wb), lambda b, h, i: (tail_blk, kcol + h)),
            pl.BlockSpec((TAIL_ROWS, wb), lambda b, h, i: (tail_blk, vcol + h)),
            const(up), const(upm),
        ],
        out_specs=pl.BlockSpec((tb, wb), lambda b, h, i: (b * nqb + i, h)),
        scratch_shapes=[pltpu.VMEM((hb, tb, HEAD_DIM), F32)],
        compiler_params=_cparams(("parallel", "parallel", "arbitrary")),
        name="prompt_attention",
    )(bias, proj, proj, proj, proj, proj, up, upm)


def _meta_attn_kernel(bias_ref, q_ref, k_ref, v_ref, up_ref, o_ref):
    bias = bias_ref[pl.program_id(0)]
    row = lax.broadcasted_iota(jnp.int32, (TAIL_ROWS, TAIL_ROWS), 0)
    col = lax.broadcasted_iota(jnp.int32, (TAIL_ROWS, TAIL_ROWS), 1)
    mask = (col < row) & (col >= META_OFF)
    run = jnp.zeros((TAIL_ROWS, 1), F32)
    acc = jnp.zeros((TAIL_ROWS, HEAD_DIM), F32)
    _, acc = _sb_block(q_ref[...].astype(BF16), k_ref[...], v_ref[...], up_ref[...], bias, run, acc, mask)
    o_ref[...] = acc


def meta_attention(proj, bias, tail_blk=ROWS_MAIN // TAIL_ROWS):
    qcol, kcol, vcol = (SSM_WIDTH // HEAD_DIM, (SSM_WIDTH + ATTN_WIDTH) // HEAD_DIM,
                        (SSM_WIDTH + 2 * ATTN_WIDTH) // HEAD_DIM)
    blk = lambda c0: pl.BlockSpec((TAIL_ROWS, HEAD_DIM), lambda h: (tail_blk, c0 + h))
    return pl.pallas_call(
        _meta_attn_kernel,
        out_shape=jax.ShapeDtypeStruct((TAIL_ROWS, ATTN_WIDTH), F32),
        grid=(N_HEADS,),
        in_specs=[pl.BlockSpec(memory_space=pltpu.SMEM), blk(qcol), blk(kcol), blk(vcol),
                  pl.BlockSpec((TAIL_ROWS, TAIL_ROWS), lambda h: (0, 0))],
        out_specs=pl.BlockSpec((TAIL_ROWS, HEAD_DIM), lambda h: (0, h)),
        compiler_params=_cparams(("parallel",)),
        name="meta_attention",
    )(bias, proj, proj, proj, jnp.asarray(_upper_matrix(TAIL_ROWS), BF16))


PAGES_PER_STEP = 8


def _sample_attn_kernel(pt_ref, q_ref, *refs, n_steps):
    npg = PAGES_PER_STEP
    k_refs = refs[:npg]
    v_refs = refs[npg:2 * npg]
    g_ref, gt_ref, up_ref, bias_ref, o_ref, run_ref = refs[2 * npg:]
    jp = pl.program_id(1)
    nh = N_HEADS
    flat = PAGE_SIZE * nh

    @pl.when(jp == 0)
    def _():
        o_ref[...] = jnp.zeros_like(o_ref)
        run_ref[...] = jnp.zeros_like(run_ref)

    q = q_ref[...].astype(BF16)
    row = lax.broadcasted_iota(jnp.int32, (nh, flat), 0)
    col = lax.broadcasted_iota(jnp.int32, (nh, flat), 1)
    own = (col & (nh - 1)) == row

    zs = []
    for p in range(npg):
        kp = k_refs[p][...].reshape(flat, HEAD_DIM).astype(BF16)
        zt = lax.dot_general(q, kp, (((1,), (1,)), ((), ())), preferred_element_type=F32)
        zs.append(jnp.where(own, zt, 0.0))
    hi, lo = _split_bf16(jnp.concatenate(zs, axis=0))
    zc = jnp.dot(jnp.concatenate([hi, lo], axis=0), g_ref[...], preferred_element_type=F32)
    z = (zc[:npg * nh] + zc[npg * nh:]) * ATTN_SCALE + bias_ref[...]
    sp = _softplus(z)
    log_1m = -sp
    hi, lo = _split_bf16(log_1m)
    ext = jnp.dot(jnp.concatenate([hi, lo], axis=0), up_ref[...], preferred_element_type=F32)
    ext = ext[:npg * nh] + ext[npg * nh:]
    after, total = ext[:, :PAGE_SIZE], ext[:, PAGE_SIZE:]
    run = run_ref[...]
    runs = []
    for p in range(npg):
        runs.append(run)
        run = run + total[p * nh:(p + 1) * nh]
    run_ref[...] = run
    a = jnp.exp((z - sp) + after + jnp.concatenate(runs, axis=0))
    aw = jnp.dot(a.astype(BF16), gt_ref[...], preferred_element_type=F32)
    acc = o_ref[...]
    for p in range(npg):
        am = jnp.where(own, aw[p * nh:(p + 1) * nh], 0.0).astype(BF16)
        vp = v_refs[p][...].reshape(flat, HEAD_DIM).astype(BF16)
        acc = acc + jnp.dot(am, vp, preferred_element_type=F32)
    o_ref[...] = acc


def sample_attention(q, cache_k, cache_v, layer, page_table, bias):
    n_b, n_pages = page_table.shape
    npg = PAGES_PER_STEP
    n_steps = n_pages // npg
    flat = PAGE_SIZE * N_HEADS
    g = (np.arange(flat)[:, None] // N_HEADS == np.arange(PAGE_SIZE)[None, :]).astype(np.float32)
    bias_rows = jnp.tile(jnp.broadcast_to(bias[:, None], (N_HEADS, PAGE_SIZE)), (npg, 1))

    def page_spec(p):
        return pl.BlockSpec((None, None, PAGE_SIZE, N_HEADS, HEAD_DIM),
                            lambda b, jp, pt: (layer, pt[b, n_pages - 1 - (jp * npg + p)], 0, 0, 0))

    const = lambda shape: pl.BlockSpec(shape, lambda b, jp, pt: (0, 0))
    head_spec = pl.BlockSpec((None, N_HEADS, HEAD_DIM), lambda b, jp, pt: (b, 0, 0))
    grid_spec = pltpu.PrefetchScalarGridSpec(
        num_scalar_prefetch=1,
        grid=(n_b, n_steps),
        in_specs=([head_spec] + [page_spec(p) for p in range(npg)] * 2
                  + [const((flat, PAGE_SIZE)), const((PAGE_SIZE, flat)),
                     const((PAGE_SIZE, PAGE_SIZE + HEAD_DIM)), const((npg * N_HEADS, PAGE_SIZE))]),
        out_specs=head_spec,
        scratch_shapes=[pltpu.VMEM((N_HEADS, HEAD_DIM), F32)],
    )
    return pl.pallas_call(
        functools.partial(_sample_attn_kernel, n_steps=n_steps),
        out_shape=jax.ShapeDtypeStruct((n_b, N_HEADS, HEAD_DIM), F32),
        grid_spec=grid_spec,
        compiler_params=_cparams(("arbitrary", "arbitrary")),
        name="sample_attention",
    )(page_table, q, *([cache_k] * npg), *([cache_v] * npg),
      jnp.asarray(g, BF16), jnp.asarray(g.T, BF16), jnp.asarray(_upper_ext_matrix(PAGE_SIZE), BF16), bias_rows)


def _ssm_params(i, lam_re, lam_im, log_step, b_re, b_im, c_re, c_im, d, w_glu, b_glu, out_gain):
    ar, ai, bbr, bbi = ssm_discretize(lam_re[i], lam_im[i], log_step[i],
                                      jnp.swapaxes(b_re[i], 1, 2), jnp.swapaxes(b_im[i], 1, 2))
    bb = jnp.concatenate([_block_diag(bbr), _block_diag(bbi)], axis=-1).astype(BF16)
    cc = jnp.concatenate([_block_diag(jnp.swapaxes(c_re[i], 1, 2)),
                          -_block_diag(jnp.swapaxes(c_im[i], 1, 2))], axis=1).astype(BF16)
    return dict(
        ar=ar.reshape(N_GB, 1, GB_ST), ai=ai.reshape(N_GB, 1, GB_ST), bb=bb, cc=cc,
        glu=_block_diag(w_glu[i]).astype(BF16),
        d=d[i].reshape(N_GB, 1, GB_CH), bglu=b_glu[i].reshape(N_GB, 1, GB_CH),
        gain=out_gain[i].reshape(1, SSM_WIDTH))


def kernel(x_prompt, x_sample, cache_k, cache_v, state_ssm_re, state_ssm_im, page_table, meta_tokens,
           ffn1_norm_pre, ffn1_norm_post, ffn1_w_gate, ffn1_w_up, ffn1_w_down,
           mix_norm_pre, mix_norm_post, w_in, sb_logit_bias,
           ssm_lambda_re, ssm_lambda_im, ssm_log_step, ssm_b_re, ssm_b_im, ssm_c_re, ssm_c_im,
           ssm_d, ssm_w_glu, ssm_b_glu, ssm_out_norm, attn_out_norm, w_out,
           ffn2_norm_pre, ffn2_norm_post, ffn2_w_gate, ffn2_w_up, ffn2_w_down):
    n_pad = TAIL_ROWS - DEC_BATCH - N_META
    tail_of = lambda sample, meta: jnp.concatenate([sample, jnp.zeros((n_pad, sample.shape[1]), sample.dtype), meta], 0)
    x = (x_prompt.reshape(ROWS_MAIN, D_MODEL), tail_of(x_sample.reshape(DEC_BATCH, D_MODEL), meta_tokens.astype(F32)))
    zero_state = jnp.zeros((SCAN_ROWS, SSM_GROUPS * SSM_STATE), F32)
    tail_blk16 = (ROWS_MAIN + META_OFF) // N_META
    kcol, vcol = SSM_WIDTH + ATTN_WIDTH, SSM_WIDTH + 2 * ATTN_WIDTH

    def ffn(x, h, w_bf16, casts, g_post, g_next):
        if w_bf16[2] is None:
            a, (w_down, *cast) = gateup(h, w_bf16[0], w_bf16[1], [w_bf16[3]] + casts)
        else:
            (a, cast), w_down = gateup(h, w_bf16[0], w_bf16[1], casts), w_bf16[2]
        f = matmul(a, w_down, tn=512)
        if g_next is None:
            return (resnorm_last(f, x, g_post, 0.5, 0, ROWS_MAIN),
                    resnorm_last(f, x, g_post, 0.5, ROWS_MAIN, TAIL_ROWS)), None, cast
        return resnorm(f, x, g_post, g_next, 0.5) + (cast,)

    ffn_casts = lambda wg, wu, wd, i: [(wg, i, True), (wu, i, True), (wd, i, False)]

    outs = {k: [] for k in ('kp', 'vp', 'srp', 'sip', 'ks', 'vs', 'srs', 'sis')}
    h = rmsnorm_rows(x[0], x[1], ffn1_norm_pre[0])
    w_ffn1 = [cast_weight(ffn1_w_gate, 0), cast_weight(ffn1_w_up, 0), None, (ffn1_w_down, 0, False)]
    for i in range(DEPTH):
        casts = [(w_in, i, False), (w_out, i, False)] + ffn_casts(ffn2_w_gate, ffn2_w_up, ffn2_w_down, i)
        x, h, (w_in_b, w_out_b, *w_ffn2) = ffn(x, h, w_ffn1, casts, ffn1_norm_post[i], mix_norm_pre[i])

        proj = matmul(h, w_in_b, tm=TM_BIG, tn=512)
        sp = _ssm_params(i, ssm_lambda_re, ssm_lambda_im, ssm_log_step, ssm_b_re, ssm_b_im,
                         ssm_c_re, ssm_c_im, ssm_d, ssm_w_glu, ssm_b_glu, ssm_out_norm)
        ys_meta, mr, mi = ssm_mixer(proj, lambda b, ti: tail_blk16, zero_state, zero_state, sp,
                                    n_seq=BATCH, tc=N_META, nt=1, perm=True)
        ys_main, pr, pi = ssm_mixer(proj, lambda b, ti: b * (SEQ // SCAN_TC) + ti, mr, mi, sp,
                                    n_seq=BATCH, tc=SCAN_TC, nt=SEQ // SCAN_TC, perm=True)
        ys_samp, sr, si = ssm_mixer(proj, lambda b, ti: ROWS_MAIN // DEC_BATCH,
                                    state_ssm_re[i].reshape(DEC_BATCH, -1), state_ssm_im[i].reshape(DEC_BATCH, -1),
                                    sp, n_seq=DEC_BATCH, tc=1, nt=1, perm=False)
        pr, pi = pr[:BATCH], pi[:BATCH]
        ys = jnp.concatenate(list(ys_main) + [ys_samp[0], jnp.zeros((n_pad, SSM_WIDTH), BF16), ys_meta[0]], axis=0)

        at_main = prompt_attention(proj, sb_logit_bias[i])
        at_meta = meta_attention(proj, sb_logit_bias[i])
        q_samp = proj[ROWS_MAIN:ROWS_MAIN + DEC_BATCH, SSM_WIDTH:SSM_WIDTH + ATTN_WIDTH]
        at_samp = sample_attention(q_samp.reshape(DEC_BATCH, N_HEADS, HEAD_DIM), cache_k, cache_v, i, page_table,
                                   sb_logit_bias[i]).reshape(DEC_BATCH, ATTN_WIDTH)
        ya = rmsnorm_rows(at_main, tail_of(at_samp, at_meta[META_OFF:]), attn_out_norm[i])

        f = matmul2(ys, ya, w_out_b)
        x, h = resnorm(f, x, mix_norm_post[i], ffn2_norm_pre[i], 1.0)

        last = i + 1 == DEPTH
        casts = [] if last else ffn_casts(ffn1_w_gate, ffn1_w_up, ffn1_w_down, i + 1)
        x, h, w_ffn1 = ffn(x, h, w_ffn2, casts, ffn2_norm_post[i], None if last else ffn1_norm_pre[i + 1])

        k_all, v_all = proj[:, kcol:kcol + ATTN_WIDTH], proj[:, vcol:vcol + ATTN_WIDTH]

        def prompt_rows(t):
            meta = jnp.broadcast_to(t[ROWS_MAIN + META_OFF:][None], (BATCH, N_META, ATTN_WIDTH))
            full = jnp.concatenate([meta, t[:ROWS_MAIN].reshape(BATCH, SEQ, ATTN_WIDTH)], axis=1)
            return full.reshape(BATCH, N_META + SEQ, N_HEADS, HEAD_DIM)

        sample_rows = lambda t: t[ROWS_MAIN:ROWS_MAIN + DEC_BATCH].reshape(DEC_BATCH, 1, N_HEADS, HEAD_DIM)
        outs['kp'].append(prompt_rows(k_all))
        outs['vp'].append(prompt_rows(v_all))
        outs['ks'].append(sample_rows(k_all))
        outs['vs'].append(sample_rows(v_all))
        outs['srp'].append(pr.reshape(BATCH, SSM_GROUPS, SSM_STATE))
        outs['sip'].append(pi.reshape(BATCH, SSM_GROUPS, SSM_STATE))
        outs['srs'].append(sr.reshape(DEC_BATCH, SSM_GROUPS, SSM_STATE))
        outs['sis'].append(si.reshape(DEC_BATCH, SSM_GROUPS, SSM_STATE))

    x_main, x_tail = x
    y_prompt = x_main.reshape(BATCH, SEQ, D_MODEL)
    y_sample = x_tail[:DEC_BATCH].reshape(DEC_BATCH, 1, D_MODEL)
    st = lambda k: jnp.stack(outs[k])
    return (y_prompt, y_sample, st('kp'), st('vp'), st('srp'), st('sip'),
            st('ks'), st('vs'), st('srs'), st('sis'))
```
